```python
import jax, jax.numpy as jnp
from jax import lax
import numpy as np

D_MODEL = 2048
BATCH = 2
SEQ = 16384
DEPTH = 4

CHUNK = 64
N_MIXERS = 2
N_MLA_LAYERS = (DEPTH + N_MIXERS - 1) // N_MIXERS
N_LRU_LAYERS = DEPTH // N_MIXERS
MLA_HEADS = 16
Q_LORA = 512
KV_LORA = 512
QK_NOPE = 128
QK_ROPE = 64
V_HEAD = 128
QK_HEAD = QK_NOPE + QK_ROPE
ROPE_THETA = 10000.0
Q_BLOCK = 128
D_RNN = 2688
LRU_BLOCKS = 16
LRU_BLOCK = D_RNN // LRU_BLOCKS
CONV_WIDTH = 4
LRU_C = 8.0
D_FF = 4 * D_MODEL
EPS = 1e-6
MAX_POS_OFFSET = 4096

kernel_name = "hybrid_mla_rglru_sqrelu_trunk"


def rmsnorm(x, g):
    xf = x.astype(jnp.float32)
    y = xf * lax.rsqrt(jnp.mean(xf * xf, axis=-1, keepdims=True) + EPS)
    return (y * g.astype(jnp.float32)).astype(x.dtype)


def rope_tables(positions):
    half = QK_ROPE // 2
    inv_freq = ROPE_THETA ** (-jnp.arange(half, dtype=jnp.float32) / half)
    ang = positions.astype(jnp.float32)[..., None] * inv_freq
    return jnp.cos(ang), jnp.sin(ang)


def apply_rope(x, cos, sin):
    xf = x.astype(jnp.float32)
    x1, x2 = jnp.split(xf, 2, axis=-1)
    return jnp.concatenate([x1 * cos - x2 * sin, x2 * cos + x1 * sin], axis=-1).astype(x.dtype)


def mla(x, positions, w_dq, g_q, w_uq, w_dkv, g_kv, w_ukv, w_o):
    B, S, _ = x.shape
    cq = rmsnorm(x @ w_dq, g_q)
    q = (cq @ w_uq).reshape(B, S, MLA_HEADS, QK_HEAD)
    q_nope, q_rope = q[..., :QK_NOPE], q[..., QK_NOPE:]
    kv_a = x @ w_dkv
    ckv = rmsnorm(kv_a[..., :KV_LORA], g_kv)
    kv = (ckv @ w_ukv).reshape(B, S, MLA_HEADS, QK_NOPE + V_HEAD)
    k_nope, v = kv[..., :QK_NOPE], kv[..., QK_NOPE:]
    cos, sin = rope_tables(positions)
    q_rope = apply_rope(q_rope, cos[:, :, None, :], sin[:, :, None, :])
    k_rope = apply_rope(kv_a[..., KV_LORA:], cos, sin)
    scale = QK_HEAD ** -0.5
    n_blocks = S // Q_BLOCK
    qn_b = q_nope.reshape(B, n_blocks, Q_BLOCK, MLA_HEADS, QK_NOPE).transpose(1, 0, 2, 3, 4)
    qr_b = q_rope.reshape(B, n_blocks, Q_BLOCK, MLA_HEADS, QK_ROPE).transpose(1, 0, 2, 3, 4)
    k_chunk = jnp.arange(S) // CHUNK
    neg = jnp.finfo(jnp.float32).min

    def attend(args):
        blk, qn, qr = args
        s = (jnp.einsum('bqhd,bkhd->bhqk', qn, k_nope)
             + jnp.einsum('bqhr,bkr->bhqk', qr, k_rope)).astype(jnp.float32) * scale
        q_chunk = (blk * Q_BLOCK + jnp.arange(Q_BLOCK)) // CHUNK
        allowed = k_chunk[None, :] <= q_chunk[:, None]
        s = jnp.where(allowed[None, None], s, neg)
        p = jax.nn.softmax(s, axis=-1).astype(v.dtype)
        return jnp.einsum('bhqk,bkhd->bqhd', p, v)

    o = lax.map(attend, (jnp.arange(n_blocks), qn_b, qr_b))
    o = o.transpose(1, 0, 2, 3, 4).reshape(B, S, MLA_HEADS * V_HEAD)
    return o @ w_o


def causal_depthwise_conv(u, w, b):
    out = lax.conv_general_dilated(
        u, w[:, None, :].astype(u.dtype), window_strides=(1,),
        padding=[(CONV_WIDTH - 1, 0)],
        dimension_numbers=('NWC', 'WIO', 'NWC'),
        feature_group_count=u.shape[-1])
    return out + b


def block_diag(u, w, b):
    B, S, _ = u.shape
    ub = u.reshape(B, S, LRU_BLOCKS, LRU_BLOCK)
    return jnp.einsum('bsnd,nde->bsne', ub, w).reshape(B, S, D_RNN) + b


def scan_combine(c1, c2):
    a1, b1 = c1
    a2, b2 = c2
    return a1 * a2, a2 * b1 + b2


def rglru_block(x, w_y, b_y, w_x, b_x, conv_w, conv_b, w_ga, b_ga, w_gi, b_gi, lam, w_out, b_out):
    y = jax.nn.gelu(x @ w_y + b_y)
    u = causal_depthwise_conv(x @ w_x + b_x, conv_w, conv_b)
    r = jax.nn.sigmoid(block_diag(u, w_ga, b_ga)).astype(jnp.float32)
    i = jax.nn.sigmoid(block_diag(u, w_gi, b_gi))
    log_a = LRU_C * r * jax.nn.log_sigmoid(lam.astype(jnp.float32))
    a = jnp.exp(log_a)
    inp = jnp.sqrt(-jnp.expm1(2.0 * log_a)) * (i * u).astype(jnp.float32)
    _, h = lax.associative_scan(scan_combine, (a, inp), axis=1)
    return (h.astype(x.dtype) * y) @ w_out + b_out


def sq_relu_mlp(x, w1, w2):
    return jnp.square(jax.nn.relu(x @ w1)) @ w2


def setup_inputs(seed: int = 0) -> dict:
    key = jax.random.key(seed)
    ks = iter(jax.random.split(key, 48))

    def nrm(shape, fan_in):
        return jax.random.normal(next(ks), shape, jnp.float32) * (fan_in ** -0.5)

    def gain(shape):
        return 1.0 + 0.02 * jax.random.normal(next(ks), shape, jnp.float32)

    def bias(shape):
        return 0.01 * jax.random.normal(next(ks), shape, jnp.float32)

    x = jax.random.normal(next(ks), (BATCH, SEQ, D_MODEL), jnp.float32)
    offset = jax.random.randint(next(ks), (BATCH, 1), 0, MAX_POS_OFFSET, dtype=jnp.int32)
    positions = (offset + jnp.arange(SEQ, dtype=jnp.int32)[None, :]).astype(jnp.int32)

    NA, NB = N_MLA_LAYERS, N_LRU_LAYERS
    a0 = jax.random.uniform(next(ks), (NB, D_RNN), jnp.float32, 0.9, 0.999)
    p = a0 ** (1.0 / LRU_C)
    lam = jnp.log(p) - jnp.log1p(-p)

    return {
        'x': x,
        'positions': positions,
        'mix_pre_g': gain((DEPTH, D_MODEL)),
        'mix_post_g': gain((DEPTH, D_MODEL)),
        'mlp_pre_g': gain((DEPTH, D_MODEL)),
        'mlp_post_g': gain((DEPTH, D_MODEL)),
        'mla_w_dq': nrm((NA, D_MODEL, Q_LORA), D_MODEL),
        'mla_g_q': gain((NA, Q_LORA)),
        'mla_w_uq': nrm((NA, Q_LORA, MLA_HEADS * QK_HEAD), Q_LORA),
        'mla_w_dkv': nrm((NA, D_MODEL, KV_LORA + QK_ROPE), D_MODEL),
        'mla_g_kv': gain((NA, KV_LORA)),
        'mla_w_ukv': nrm((NA, KV_LORA, MLA_HEADS * (QK_NOPE + V_HEAD)), KV_LORA),
        'mla_w_o': nrm((NA, MLA_HEADS * V_HEAD, D_MODEL), MLA_HEADS * V_HEAD),
        'lru_w_y': nrm((NB, D_MODEL, D_RNN), D_MODEL),
        'lru_b_y': bias((NB, D_RNN)),
        'lru_w_x': nrm((NB, D_MODEL, D_RNN), D_MODEL),
        'lru_b_x': bias((NB, D_RNN)),
        'lru_conv_w': nrm((NB, CONV_WIDTH, D_RNN), CONV_WIDTH),
        'lru_conv_b': bias((NB, D_RNN)),
        'lru_w_ga': nrm((NB, LRU_BLOCKS, LRU_BLOCK, LRU_BLOCK), LRU_BLOCK),
        'lru_b_ga': bias((NB, D_RNN)),
        'lru_w_gi': nrm((NB, LRU_BLOCKS, LRU_BLOCK, LRU_BLOCK), LRU_BLOCK),
        'lru_b_gi': bias((NB, D_RNN)),
        'lru_lam': lam,
        'lru_w_out': nrm((NB, D_RNN, D_MODEL), D_RNN),
        'lru_b_out': bias((NB, D_MODEL)),
        'mlp_w1': nrm((DEPTH, D_MODEL, D_FF), D_MODEL),
        'mlp_w2': nrm((DEPTH, D_FF, D_MODEL), D_FF),
    }


def reference(x, positions, mix_pre_g, mix_post_g, mlp_pre_g, mlp_post_g,
              mla_w_dq, mla_g_q, mla_w_uq, mla_w_dkv, mla_g_kv, mla_w_ukv, mla_w_o,
              lru_w_y, lru_b_y, lru_w_x, lru_b_x, lru_conv_w, lru_conv_b,
              lru_w_ga, lru_b_ga, lru_w_gi, lru_b_gi, lru_lam, lru_w_out, lru_b_out,
              mlp_w1, mlp_w2):
    for i in range(DEPTH):
        j = i // N_MIXERS
        h = rmsnorm(x, mix_pre_g[i])
        if i % N_MIXERS == 0:
            h = mla(h, positions, mla_w_dq[j], mla_g_q[j], mla_w_uq[j], mla_w_dkv[j],
                    mla_g_kv[j], mla_w_ukv[j], mla_w_o[j])
        else:
            h = rglru_block(h, lru_w_y[j], lru_b_y[j], lru_w_x[j], lru_b_x[j],
                            lru_conv_w[j], lru_conv_b[j], lru_w_ga[j], lru_b_ga[j],
                            lru_w_gi[j], lru_b_gi[j], lru_lam[j], lru_w_out[j], lru_b_out[j])
        x = x + rmsnorm(h, mix_post_g[i])
        h = rmsnorm(x, mlp_pre_g[i])
        h = sq_relu_mlp(h, mlp_w1[i], mlp_w2[i])
        x = x + rmsnorm(h, mlp_post_g[i])
    return x
```

```python
import functools
import math

import jax
import jax.numpy as jnp
from jax import lax
from jax.experimental import pallas as pl
from jax.experimental.pallas import tpu as pltpu

MLA_HEADS = 16
QK_NOPE = 128
QK_ROPE = 64
V_HEAD = 128
QK_HEAD = QK_NOPE + QK_ROPE
ROPE_THETA = 10000.0
CHUNK = 64
LRU_BLOCKS = 16
CONV_WIDTH = 4
LRU_C = 8.0
EPS = 1e-6
N_MIXERS = 2

LANES = 128
SUBLANES = 8
BF16_SUBLANES = 16
V7X_SCOPED_VMEM_BYTES = 60000 * 1024

HEAD_PAD = 2 * LANES
VT_ROWS = V_HEAD + BF16_SUBLANES
NEG_BIG = -1e30

F32 = jnp.float32
BF16 = jnp.bfloat16


def _cparams(semantics):
    return pltpu.CompilerParams(dimension_semantics=semantics,
                                vmem_limit_bytes=V7X_SCOPED_VMEM_BYTES)


def _resident(shape):
    nd = len(shape)
    return pl.BlockSpec(shape, lambda *_: (0,) * nd, pipeline_mode=pl.Buffered(1))


def _rms(x, g):
    y = x * lax.rsqrt(jnp.mean(x * x, axis=-1, keepdims=True) + EPS)
    return y * g


def _sigmoid(x):
    return 1.0 / (1.0 + jnp.exp(-x))


def _gelu_tanh(x):
    c = math.sqrt(2.0 / math.pi)
    return 0.5 * x * (1.0 + jnp.tanh(c * (x + 0.044715 * (x * x * x))))


def _prenorm_kernel(x_ref, g_ref, o_ref):
    o_ref[...] = _rms(x_ref[...], g_ref[...]).astype(o_ref.dtype)


def _prenorm(x2d, g, tm):
    m, d = x2d.shape
    return pl.pallas_call(
        _prenorm_kernel,
        grid=(m // tm,),
        in_specs=[pl.BlockSpec((tm, d), lambda i: (i, 0)), _resident((1, d))],
        out_specs=pl.BlockSpec((tm, d), lambda i: (i, 0)),
        out_shape=jax.ShapeDtypeStruct((m, d), BF16),
        compiler_params=_cparams(("parallel",)),
        name="prenorm",
    )(x2d, g)


def _mla_proj_kernel(xn_ref, pos_ref, invf_ref, wa_ref, gq_ref, gkv_ref,
                     wqn_ref, wqr_ref, wqrr_ref, wkn_ref, wvt_ref,
                     q_ref, k_ref, vt_ref, *, qscale, q_lora, kv_lora, head_group):
    xn = xn_ref[0]
    a = jnp.dot(xn, wa_ref[...], preferred_element_type=F32)
    cq = _rms(a[:, :q_lora], gq_ref[...]).astype(BF16)
    ckv = _rms(a[:, q_lora:q_lora + kv_lora], gkv_ref[...]).astype(BF16)
    ang = pos_ref[0].astype(F32) * invf_ref[...]
    cos = jnp.cos(ang)
    sin = jnp.sin(ang)
    r0 = q_lora + kv_lora
    kr = (a[:, r0:r0 + LANES] * cos + a[:, r0 + LANES:r0 + 2 * LANES] * sin).astype(BF16)
    cs = cos * qscale
    sn = sin * qscale
    ones = jnp.ones((VT_ROWS - V_HEAD, vt_ref.shape[-1]), BF16)
    gw = head_group * LANES
    for g in range(MLA_HEADS // head_group):
        cols = slice(g * gw, (g + 1) * gw)
        qn = jnp.dot(cq, wqn_ref[:, cols], preferred_element_type=F32)
        qr = jnp.dot(cq, wqr_ref[:, cols], preferred_element_type=F32)
        qrr = jnp.dot(cq, wqrr_ref[:, cols], preferred_element_type=F32)
        kn = jnp.dot(ckv, wkn_ref[:, cols], preferred_element_type=F32)
        vt = lax.dot_general(wvt_ref[cols, :], ckv, (((1,), (1,)), ((), ())),
                             preferred_element_type=F32)
        for j in range(head_group):
            h = g * head_group + j
            sl = slice(j * LANES, (j + 1) * LANES)
            q_ref[0, h, :, 0:LANES] = (qn[:, sl] * qscale).astype(BF16)
            q_ref[0, h, :, LANES:HEAD_PAD] = (qr[:, sl] * cs + qrr[:, sl] * sn).astype(BF16)
            k_ref[0, h, :, 0:LANES] = kn[:, sl].astype(BF16)
            k_ref[0, h, :, LANES:HEAD_PAD] = kr
            vt_ref[0, h, 0, 0:V_HEAD, :] = vt[sl, :].astype(BF16)
            vt_ref[0, h, 0, V_HEAD:VT_ROWS, :] = ones


def _mla_proj(xn, pos3, invf, wa, gq, gkv, wqn, wqr, wqrr, wkn, wvt, tk):
    b, s, d = xn.shape
    q_lora = gq.shape[-1]
    kv_lora = gkv.shape[-1]
    qscale = (QK_HEAD ** -0.5) * math.log2(math.e)
    kern = functools.partial(_mla_proj_kernel, qscale=qscale, q_lora=q_lora,
                             kv_lora=kv_lora, head_group=4)
    hq = MLA_HEADS
    return pl.pallas_call(
        kern,
        grid=(b, s // tk),
        in_specs=[
            pl.BlockSpec((1, tk, d), lambda bi, i: (bi, i, 0)),
            pl.BlockSpec((1, tk, 1), lambda bi, i: (bi, i, 0)),
            _resident(invf.shape), _resident(wa.shape), _resident(gq.shape),
            _resident(gkv.shape), _resident(wqn.shape), _resident(wqr.shape),
            _resident(wqrr.shape), _resident(wkn.shape), _resident(wvt.shape),
        ],
        out_specs=[
            pl.BlockSpec((1, hq, tk, HEAD_PAD), lambda bi, i: (bi, 0, i, 0)),
            pl.BlockSpec((1, hq, tk, HEAD_PAD), lambda bi, i: (bi, 0, i, 0)),
            pl.BlockSpec((1, hq, 1, VT_ROWS, tk), lambda bi, i: (bi, 0, i, 0, 0)),
        ],
        out_shape=[
            jax.ShapeDtypeStruct((b, hq, s, HEAD_PAD), BF16),
            jax.ShapeDtypeStruct((b, hq, s, HEAD_PAD), BF16),
            jax.ShapeDtypeStruct((b, hq, s // tk, VT_ROWS, tk), BF16),
        ],
        compiler_params=_cparams(("parallel", "parallel")),
        name="mla_proj",
    )(xn, pos3, invf, wa, gq, gkv, wqn, wqr, wqrr, wkn, wvt)


def _attn_kernel(q_ref, k_ref, vt_ref, o_ref, acc_ref, m_ref, *, tq, tk):
    qi = pl.program_id(2)
    q = q_ref[0, 0]
    acc_ref[...] = jnp.zeros(acc_ref.shape, F32)
    m_ref[...] = jnp.full(m_ref.shape, NEG_BIG, F32)
    n_full = qi * (tq // tk)

    def chunk(ki, masked):
        k0 = pl.multiple_of(ki * tk, tk)
        kb = k_ref[0, 0, pl.ds(k0, tk), :]
        s = lax.dot_general(kb, q, (((1,), (1,)), ((), ())),
                            preferred_element_type=F32)
        if masked:
            kc = (k0 + lax.broadcasted_iota(jnp.int32, (tk, tq), 0)) // CHUNK
            qc = (qi * tq + lax.broadcasted_iota(jnp.int32, (tk, tq), 1)) // CHUNK
            s = jnp.where(kc <= qc, s, NEG_BIG)
        m_old = m_ref[0:1, :]
        m_new = jnp.maximum(m_old, jnp.max(s, axis=0, keepdims=True))
        alpha = jnp.exp2(m_old - m_new)
        p = jnp.exp2(s - m_new).astype(BF16)
        pv = jnp.dot(vt_ref[0, 0, ki], p, preferred_element_type=F32)
        acc_ref[...] = acc_ref[...] * alpha + pv
        m_ref[0:1, :] = m_new

    def full_body(ki, carry):
        chunk(ki, False)
        return carry

    lax.fori_loop(0, n_full, full_body, 0)
    for j in range(tq // tk):
        chunk(n_full + j, True)
    acc = acc_ref[...]
    o = acc[0:V_HEAD, :] / acc[V_HEAD:V_HEAD + 1, :]
    o_ref[0] = o.T.astype(o_ref.dtype)


def _attention(q, k, vt, tq, tk):
    b, h, s, _ = q.shape
    kern = functools.partial(_attn_kernel, tq=tq, tk=tk)
    return pl.pallas_call(
        kern,
        grid=(b, h, s // tq),
        in_specs=[
            pl.BlockSpec((1, 1, tq, HEAD_PAD), lambda bi, hi, qi: (bi, hi, qi, 0)),
            pl.BlockSpec((1, 1, s, HEAD_PAD), lambda bi, hi, qi: (bi, hi, 0, 0)),
            pl.BlockSpec((1, 1, s // tk, VT_ROWS, tk), lambda bi, hi, qi: (bi, hi, 0, 0, 0)),
        ],
        out_specs=pl.BlockSpec((1, tq, V_HEAD), lambda bi, hi, qi: (bi, qi, hi)),
        out_shape=jax.ShapeDtypeStruct((b, s, h * V_HEAD), BF16),
        scratch_shapes=[pltpu.VMEM((VT_ROWS, tq), F32), pltpu.VMEM((SUBLANES, tq), F32)],
        compiler_params=_cparams(("parallel", "parallel", "arbitrary")),
        name="mla_attention",
    )(q, k, vt)


def _out_proj_kernel(a_ref, w_ref, b_ref, x_ref, gpost_ref, gnext_ref, x1_ref, xn_ref):
    h = jnp.dot(a_ref[...], w_ref[...], preferred_element_type=F32) + b_ref[...]
    x1 = x_ref[...] + _rms(h, gpost_ref[...])
    x1_ref[...] = x1
    xn_ref[...] = _rms(x1, gnext_ref[...]).astype(xn_ref.dtype)


def _out_proj(a2d, w, bias, x2d, gpost, gnext, tm):
    m, kdim = a2d.shape
    d = w.shape[-1]
    return pl.pallas_call(
        _out_proj_kernel,
        grid=(m // tm,),
        in_specs=[
            pl.BlockSpec((tm, kdim), lambda i: (i, 0)),
            _resident(w.shape), _resident((1, d)),
            pl.BlockSpec((tm, d), lambda i: (i, 0)),
            _resident((1, d)), _resident((1, d)),
        ],
        out_specs=[pl.BlockSpec((tm, d), lambda i: (i, 0)),
                   pl.BlockSpec((tm, d), lambda i: (i, 0))],
        out_shape=[jax.ShapeDtypeStruct((m, d), F32), jax.ShapeDtypeStruct((m, d), BF16)],
        compiler_params=_cparams(("parallel",)),
        name="out_proj",
    )(a2d, w, bias, x2d, gpost, gnext)


def _mlp_kernel(xn_ref, w1_ref, w2_ref, x_ref, gpost_ref, gnext_ref, x2_ref, xn2_ref, acc_ref):
    f = pl.program_id(1)

    @pl.when(f == 0)
    def _():
        acc_ref[...] = jnp.zeros(acc_ref.shape, F32)

    a = jnp.dot(xn_ref[...], w1_ref[...], preferred_element_type=F32)
    a = jnp.maximum(a, 0.0)
    a = (a * a).astype(BF16)
    acc_ref[...] += jnp.dot(a, w2_ref[...], preferred_element_type=F32)

    @pl.when(f == pl.num_programs(1) - 1)
    def _():
        x2 = x_ref[...] + _rms(acc_ref[...], gpost_ref[...])
        x2_ref[...] = x2
        xn2_ref[...] = _rms(x2, gnext_ref[...]).astype(xn2_ref.dtype)


def _mlp(xn2d, w1, w2, x2d, gpost, gnext, tm, tf):
    m, d = xn2d.shape
    ff = w1.shape[-1]
    return pl.pallas_call(
        _mlp_kernel,
        grid=(m // tm, ff // tf),
        in_specs=[
            pl.BlockSpec((tm, d), lambda i, f: (i, 0)),
            pl.BlockSpec((d, tf), lambda i, f: (0, f)),
            pl.BlockSpec((tf, d), lambda i, f: (f, 0)),
            pl.BlockSpec((tm, d), lambda i, f: (i, 0)),
            _resident((1, d)), _resident((1, d)),
        ],
        out_specs=[pl.BlockSpec((tm, d), lambda i, f: (i, 0)),
                   pl.BlockSpec((tm, d), lambda i, f: (i, 0))],
        out_shape=[jax.ShapeDtypeStruct((m, d), F32), jax.ShapeDtypeStruct((m, d), BF16)],
        scratch_shapes=[pltpu.VMEM((tm, d), F32)],
        compiler_params=_cparams(("parallel", "arbitrary")),
        name="sq_relu_mlp",
    )(xn2d, w1, w2, x2d, gpost, gnext)


def _lru_in_kernel(xn_ref, wy_ref, by_ref, wx_ref, bx_ref, y_ref, u_ref):
    xn = xn_ref[...]
    zy = jnp.dot(xn, wy_ref[...], preferred_element_type=F32) + by_ref[...]
    y_ref[...] = _gelu_tanh(zy).astype(y_ref.dtype)
    u_ref[...] = jnp.dot(xn, wx_ref[...], preferred_element_type=F32) + bx_ref[...]


def _lru_in(xn2d, wy, by, wx, bx, tm):
    m, d = xn2d.shape
    c = wy.shape[-1]
    return pl.pallas_call(
        _lru_in_kernel,
        grid=(m // tm,),
        in_specs=[pl.BlockSpec((tm, d), lambda i: (i, 0)),
                  _resident(wy.shape), _resident((1, c)),
                  _resident(wx.shape), _resident((1, c))],
        out_specs=[pl.BlockSpec((tm, c), lambda i: (i, 0)),
                   pl.BlockSpec((tm, c), lambda i: (i, 0))],
        out_shape=[jax.ShapeDtypeStruct((m, c), BF16), jax.ShapeDtypeStruct((m, c), F32)],
        compiler_params=_cparams(("parallel",)),
        name="lru_in_proj",
    )(xn2d, wy, by, wx, bx)


GATE_TILE = 3 * LANES


def _gate_windows(c, nblk):
    blk = c // nblk
    wins = []
    for j in range(c // GATE_TILE):
        c0, c1 = j * GATE_TILE, (j + 1) * GATE_TILE
        n0, n1 = c0 // blk, (c1 - 1) // blk
        lo = (n0 * blk) // LANES * LANES
        hi = -(-((n1 + 1) * blk) // LANES) * LANES
        wins.append((lo, hi))
    return tuple(wins)


def _lru_core_kernel(*refs, wins, ts):
    nw = len(wins)
    (u_ref, y_ref, cw_ref, cb_ref, bga_ref, bgi_ref, lam_ref) = refs[:7]
    wg_refs = refs[7:7 + nw]
    g_ref = refs[7 + nw]
    ext_s, a_s, b_s, hc_s = refs[8 + nw:]
    c = u_ref.shape[-1]
    t = pl.program_id(1)

    @pl.when(t == 0)
    def _():
        ext_s[0:SUBLANES, :] = jnp.zeros((SUBLANES, c), F32)
        hc_s[...] = jnp.zeros(hc_s.shape, F32)

    up = u_ref[0]
    ext_s[SUBLANES:SUBLANES + ts, :] = up
    cw = cw_ref[...]
    u = cb_ref[...] + cw[CONV_WIDTH - 1:CONV_WIDTH, :] * up
    for kk in range(1, CONV_WIDTH):
        u = u + cw[CONV_WIDTH - 1 - kk:CONV_WIDTH - kk, :] * ext_s[pl.ds(SUBLANES - kk, ts), :]
    ext_s[0:SUBLANES, :] = ext_s[ts:ts + SUBLANES, :]

    ub = u.astype(BF16)
    za, zi = [], []
    for j, (lo, hi) in enumerate(wins):
        z = jnp.dot(ub[:, lo:hi], wg_refs[j][...], preferred_element_type=F32)
        za.append(z[:, :GATE_TILE])
        zi.append(z[:, GATE_TILE:])
    r = _sigmoid(jnp.concatenate(za, axis=1) + bga_ref[...])
    gi = _sigmoid(jnp.concatenate(zi, axis=1) + bgi_ref[...])
    lam = lam_ref[...]
    log_sig = jnp.minimum(lam, 0.0) - jnp.log1p(jnp.exp(-jnp.abs(lam)))
    log_a = (LRU_C * r) * log_sig
    a = jnp.exp(log_a)
    a_s[...] = a
    b_s[...] = jnp.sqrt(-jnp.tanh(log_a) * (a * a + 1.0)) * (gi * u)

    row = lax.broadcasted_iota(jnp.int32, (SUBLANES, c), 0)

    def body(g, hprev):
        r0 = pl.multiple_of(g * SUBLANES, SUBLANES)
        av = a_s[pl.ds(r0, SUBLANES), :]
        bv = b_s[pl.ds(r0, SUBLANES), :]
        for dd in (1, 2, 4):
            ash = pltpu.roll(av, dd, axis=0)
            bsh = pltpu.roll(bv, dd, axis=0)
            keep = row >= dd
            bv = jnp.where(keep, av * bsh + bv, bv)
            av = jnp.where(keep, av * ash, av)
        hv = av * hprev + bv
        b_s[pl.ds(r0, SUBLANES), :] = hv
        return hv[SUBLANES - 1:SUBLANES, :]

    hlast = lax.fori_loop(0, ts // SUBLANES, body, hc_s[0:1, :])
    hc_s[0:1, :] = hlast
    g_ref[0] = (b_s[...] * y_ref[0].astype(F32)).astype(g_ref.dtype)


def _lru_core(u_pre, y, cw, cb, bga, bgi, lam, wgs, wins, ts):
    b, s, c = u_pre.shape
    kern = functools.partial(_lru_core_kernel, wins=wins, ts=ts)
    row_spec = pl.BlockSpec((1, ts, c), lambda bi, t: (bi, t, 0))
    return pl.pallas_call(
        kern,
        grid=(b, s // ts),
        in_specs=[row_spec, row_spec, _resident(cw.shape), _resident((1, c)),
                  _resident((1, c)), _resident((1, c)), _resident((1, c))]
                 + [_resident(w.shape) for w in wgs],
        out_specs=row_spec,
        out_shape=jax.ShapeDtypeStruct((b, s, c), BF16),
        scratch_shapes=[pltpu.VMEM((ts + SUBLANES, c), F32), pltpu.VMEM((ts, c), F32),
                        pltpu.VMEM((ts, c), F32), pltpu.VMEM((SUBLANES, c), F32)],
        compiler_params=_cparams(("parallel", "arbitrary")),
        name="lru_core",
    )(u_pre, y, cw, cb, bga, bgi, lam, *wgs)


def _prep_mla(w_dq, w_uq, w_dkv, w_ukv, w_o):
    d = w_dq.shape[0]
    q_lora = w_dq.shape[1]
    kv_lora = w_dkv.shape[1] - QK_ROPE
    half = QK_ROPE // 2
    zpad = jnp.zeros((d, LANES - QK_ROPE), w_dkv.dtype)
    kr = w_dkv[:, kv_lora:]
    kr_rot = jnp.concatenate([-kr[:, half:], kr[:, :half]], axis=1)
    wa = jnp.concatenate([w_dq, w_dkv[:, :kv_lora], kr, zpad, kr_rot, zpad], axis=1).astype(BF16)

    wq = w_uq.reshape(q_lora, MLA_HEADS, QK_HEAD)
    zq = jnp.zeros((q_lora, MLA_HEADS, LANES - QK_ROPE), w_uq.dtype)
    qr = wq[:, :, QK_NOPE:]
    qr_rot = jnp.concatenate([-qr[:, :, half:], qr[:, :, :half]], axis=2)
    wqn = wq[:, :, :QK_NOPE].reshape(q_lora, MLA_HEADS * LANES).astype(BF16)
    wqr = jnp.concatenate([qr, zq], axis=2).reshape(q_lora, MLA_HEADS * LANES).astype(BF16)
    wqrr = jnp.concatenate([qr_rot, zq], axis=2).reshape(q_lora, MLA_HEADS * LANES).astype(BF16)

    wkv = w_ukv.reshape(kv_lora, MLA_HEADS, QK_NOPE + V_HEAD)
    wkn = wkv[:, :, :QK_NOPE].reshape(kv_lora, MLA_HEADS * QK_NOPE).astype(BF16)
    wvt = wkv[:, :, QK_NOPE:].reshape(kv_lora, MLA_HEADS * V_HEAD).T.astype(BF16)
    return wa, wqn, wqr, wqrr, wkn, wvt, w_o.astype(BF16)


def _prep_gates(w_ga, w_gi, wins):
    nblk, blk, _ = w_ga.shape
    c = nblk * blk
    eye = jnp.eye(nblk, dtype=w_ga.dtype)

    def dense(w):
        return (eye[:, None, :, None] * w[:, :, None, :]).reshape(c, c)

    da, di = dense(w_ga), dense(w_gi)
    out = []
    for j, (lo, hi) in enumerate(wins):
        cols = slice(j * GATE_TILE, (j + 1) * GATE_TILE)
        out.append(jnp.concatenate([da[lo:hi, cols], di[lo:hi, cols]], axis=1).astype(BF16))
    return out


def _pick(n, pref):
    t = min(n, pref)
    while n % t:
        t //= 2
    return t


def kernel(x, positions, mix_pre_g, mix_post_g, mlp_pre_g, mlp_post_g, mla_w_dq, mla_g_q, mla_w_uq, mla_w_dkv, mla_g_kv, mla_w_ukv, mla_w_o, lru_w_y, lru_b_y, lru_w_x, lru_b_x, lru_conv_w, lru_conv_b, lru_w_ga, lru_b_ga, lru_w_gi, lru_b_gi, lru_lam, lru_w_out, lru_b_out, mlp_w1, mlp_w2):
    b, s, d = x.shape
    depth = mix_pre_g.shape[0]
    m = b * s
    c = lru_w_y.shape[-1]
    tm = _pick(m, 512)
    tk = _pick(s, 512)
    tq = _pick(s, 1024)
    ts = _pick(s, 256)
    tf = _pick(mlp_w1.shape[-1], 1024)
    wins = _gate_windows(c, LRU_BLOCKS)

    half = QK_ROPE // 2
    inv_freq = ROPE_THETA ** (-jnp.arange(half, dtype=F32) / half)
    invf = jnp.tile(inv_freq, LANES // half)[None, :]
    pos3 = positions[:, :, None]
    zero_bias = jnp.zeros((1, d), F32)

    def row(v):
        return v[None, :].astype(F32)

    x2d = x.reshape(m, d)
    xn = _prenorm(x2d, row(mix_pre_g[0]), tm)
    for i in range(depth):
        j = i // N_MIXERS
        if i % N_MIXERS == 0:
            wa, wqn, wqr, wqrr, wkn, wvt, wo = _prep_mla(
                mla_w_dq[j], mla_w_uq[j], mla_w_dkv[j], mla_w_ukv[j], mla_w_o[j])
            q, k, vt = _mla_proj(xn.reshape(b, s, d), pos3, invf, wa, row(mla_g_q[j]),
                                 row(mla_g_kv[j]), wqn, wqr, wqrr, wkn, wvt, tk)
            o = _attention(q, k, vt, tq, tk)
            x2d, xn = _out_proj(o.reshape(m, MLA_HEADS * V_HEAD), wo, zero_bias, x2d,
                                row(mix_post_g[i]), row(mlp_pre_g[i]), tm)
        else:
            y, u_pre = _lru_in(xn, lru_w_y[j].astype(BF16), row(lru_b_y[j]),
                               lru_w_x[j].astype(BF16), row(lru_b_x[j]), tm)
            wgs = _prep_gates(lru_w_ga[j], lru_w_gi[j], wins)
            g = _lru_core(u_pre.reshape(b, s, c), y.reshape(b, s, c), lru_conv_w[j],
                          row(lru_conv_b[j]), row(lru_b_ga[j]), row(lru_b_gi[j]),
                          row(lru_lam[j]), wgs, wins, ts)
            x2d, xn = _out_proj(g.reshape(m, c), lru_w_out[j].astype(BF16), row(lru_b_out[j]),
                                x2d, row(mix_post_g[i]), row(mlp_pre_g[i]), tm)
        g_next = mix_pre_g[i + 1] if i + 1 < depth else mix_pre_g[0]
        x2d, xn = _mlp(xn, mlp_w1[i].astype(BF16), mlp_w2[i].astype(BF16), x2d,
                       row(mlp_post_g[i]), row(g_next), tm, tf)
    return x2d.reshape(b, s, d)
```

```python
import functools
import math

import jax
import jax.numpy as jnp
from jax import lax
from jax.experimental import pallas as pl
from jax.experimental.pallas import tpu as pltpu

MLA_HEADS = 16
QK_NOPE = 128
QK_ROPE = 64
V_HEAD = 128
QK_HEAD = QK_NOPE + QK_ROPE
ROPE_THETA = 10000.0
CHUNK = 64
LRU_BLOCKS = 16
CONV_WIDTH = 4
LRU_C = 8.0
EPS = 1e-6
N_MIXERS = 2

LANES = 128
SUBLANES = 8
BF16_SUBLANES = 16
V7X_SCOPED_VMEM_BYTES = 60000 * 1024

HEAD_PAD = 2 * LANES
VT_ROWS = V_HEAD + BF16_SUBLANES
NEG_BIG = -1e30

F32 = jnp.float32
BF16 = jnp.bfloat16


def _cparams(semantics):
    return pltpu.CompilerParams(dimension_semantics=semantics,
                                vmem_limit_bytes=V7X_SCOPED_VMEM_BYTES)


def _resident(shape):
    nd = len(shape)
    return pl.BlockSpec(shape, lambda *_: (0,) * nd, pipeline_mode=pl.Buffered(1))


def _layer(shape, layer):
    nd = len(shape) - 1
    return pl.BlockSpec((None,) + tuple(shape[1:]), lambda *_: (layer,) + (0,) * nd,
                        pipeline_mode=pl.Buffered(1))


def _rms(x, g):
    y = x * lax.rsqrt(jnp.mean(x * x, axis=-1, keepdims=True) + EPS)
    return y * g


def _sigmoid(x):
    return 0.5 * jnp.tanh(0.5 * x) + 0.5


def _gelu_tanh(x):
    c = math.sqrt(2.0 / math.pi)
    return 0.5 * x * (1.0 + jnp.tanh(c * (x + 0.044715 * (x * x * x))))


def _prenorm_kernel(x_ref, g_ref, o_ref):
    o_ref[...] = _rms(x_ref[...], g_ref[...]).astype(o_ref.dtype)


def _prenorm(x2d, g, tm):
    m, d = x2d.shape
    return pl.pallas_call(
        _prenorm_kernel,
        grid=(m // tm,),
        in_specs=[pl.BlockSpec((tm, d), lambda i: (i, 0)), _resident((1, d))],
        out_specs=pl.BlockSpec((tm, d), lambda i: (i, 0)),
        out_shape=jax.ShapeDtypeStruct((m, d), BF16),
        compiler_params=_cparams(("parallel",)),
        name="prenorm",
    )(x2d, g)


def _mla_proj_kernel(xn_ref, pos_ref, invf_ref, wa_ref, gq_ref, gkv_ref,
                     wqn_ref, wqr_ref, wqrr_ref, wkn_ref, wvt_ref,
                     q_ref, k_ref, vt_ref, *, qscale, q_lora, kv_lora, head_group):
    xn = xn_ref[...]
    a = jnp.dot(xn, wa_ref[...], preferred_element_type=F32)
    cq = _rms(a[:, :q_lora], gq_ref[...]).astype(BF16)
    ckv = _rms(a[:, q_lora:q_lora + kv_lora], gkv_ref[...]).astype(BF16)
    ang = pos_ref[...].astype(F32) * invf_ref[...]
    cos = jnp.cos(ang)
    sin = jnp.sin(ang)
    r0 = q_lora + kv_lora
    kr = (a[:, r0:r0 + LANES] * cos + a[:, r0 + LANES:r0 + 2 * LANES] * sin).astype(BF16)
    cs = cos * qscale
    sn = sin * qscale
    ones = jnp.ones((VT_ROWS - V_HEAD, vt_ref.shape[-1]), BF16)
    gw = head_group * LANES
    for g in range(MLA_HEADS // head_group):
        cols = slice(g * gw, (g + 1) * gw)
        qn = jnp.dot(cq, wqn_ref[:, cols], preferred_element_type=F32)
        qr = jnp.dot(cq, wqr_ref[:, cols], preferred_element_type=F32)
        qrr = jnp.dot(cq, wqrr_ref[:, cols], preferred_element_type=F32)
        kn = jnp.dot(ckv, wkn_ref[:, cols], preferred_element_type=F32)
        vt = lax.dot_general(wvt_ref[cols, :], ckv, (((1,), (1,)), ((), ())),
                             preferred_element_type=F32)
        for j in range(head_group):
            h = g * head_group + j
            sl = slice(j * LANES, (j + 1) * LANES)
            q_ref[0, h, :, 0:LANES] = (qn[:, sl] * qscale).astype(BF16)
            q_ref[0, h, :, LANES:HEAD_PAD] = (qr[:, sl] * cs + qrr[:, sl] * sn).astype(BF16)
            k_ref[0, h, :, 0:LANES] = kn[:, sl].astype(BF16)
            k_ref[0, h, :, LANES:HEAD_PAD] = kr
            vt_ref[0, h, 0, 0:V_HEAD, :] = vt[sl, :].astype(BF16)
            vt_ref[0, h, 0, V_HEAD:VT_ROWS, :] = ones


def _mla_proj(xn2d, pos2d, b, invf, wa, gq, gkv, wqn, wqr, wqrr, wkn, wvt, tk):
    m, d = xn2d.shape
    s = m // b
    nt = s // tk
    q_lora = gq.shape[-1]
    kv_lora = gkv.shape[-1]
    qscale = (QK_HEAD ** -0.5) * math.log2(math.e)
    kern = functools.partial(_mla_proj_kernel, qscale=qscale, q_lora=q_lora,
                             kv_lora=kv_lora, head_group=4)
    hq = MLA_HEADS
    return pl.pallas_call(
        kern,
        grid=(b, s // tk),
        in_specs=[
            pl.BlockSpec((tk, d), lambda bi, i: (bi * nt + i, 0)),
            pl.BlockSpec((tk, 1), lambda bi, i: (bi * nt + i, 0)),
            _resident(invf.shape), _resident(wa.shape), _resident(gq.shape),
            _resident(gkv.shape), _resident(wqn.shape), _resident(wqr.shape),
            _resident(wqrr.shape), _resident(wkn.shape), _resident(wvt.shape),
        ],
        out_specs=[
            pl.BlockSpec((1, hq, tk, HEAD_PAD), lambda bi, i: (bi, 0, i, 0)),
            pl.BlockSpec((1, hq, tk, HEAD_PAD), lambda bi, i: (bi, 0, i, 0)),
            pl.BlockSpec((1, hq, 1, VT_ROWS, tk), lambda bi, i: (bi, 0, i, 0, 0)),
        ],
        out_shape=[
            jax.ShapeDtypeStruct((b, hq, s, HEAD_PAD), BF16),
            jax.ShapeDtypeStruct((b, hq, s, HEAD_PAD), BF16),
            jax.ShapeDtypeStruct((b, hq, s // tk, VT_ROWS, tk), BF16),
        ],
        compiler_params=_cparams(("parallel", "parallel")),
        name="mla_proj",
    )(xn2d, pos2d, invf, wa, gq, gkv, wqn, wqr, wqrr, wkn, wvt)


def _attn_kernel(q_ref, k_ref, vt_ref, o_ref, acc_ref, m_ref, s_ref, *, tq, tk):
    qi = pl.program_id(2)
    r = tq // tk
    acc_ref[...] = jnp.zeros(acc_ref.shape, F32)
    m_ref[...] = jnp.full(m_ref.shape, NEG_BIG, F32)
    n_full = qi * r

    def qk(ki, slot, c0):
        k0 = pl.multiple_of(ki * tk, tk)
        kb = k_ref[0, 0, pl.ds(k0, tk), :]
        s_ref[slot, :, c0:tq] = lax.dot_general(
            kb, q_ref[0, 0, c0:tq, :], (((1,), (1,)), ((), ())),
            preferred_element_type=F32)

    def softmax_pv(ki, slot, j):
        c0 = 0 if j is None else j * tk
        s = s_ref[slot, :, c0:tq]
        if j is not None:
            kc = (j * tk + lax.broadcasted_iota(jnp.int32, s.shape, 0)) // CHUNK
            qc = (c0 + lax.broadcasted_iota(jnp.int32, s.shape, 1)) // CHUNK
            s = jnp.where(kc <= qc, s, NEG_BIG)
        m_old = m_ref[0:1, c0:tq]
        m_new = jnp.maximum(m_old, jnp.max(s, axis=0, keepdims=True))
        alpha = jnp.exp2(m_old - m_new)
        p = jnp.exp2(s - m_new).astype(BF16)
        pv = jnp.dot(vt_ref[0, 0, ki], p, preferred_element_type=F32)
        acc_ref[:, c0:tq] = acc_ref[:, c0:tq] * alpha + pv
        m_ref[0:1, c0:tq] = m_new

    qk(0, 0, 0)

    def group(g, carry):
        for u in range(r):
            qk(g * r + u + 1, (u + 1) % 2, 0)
            softmax_pv(g * r + u, u % 2, None)
        return carry

    lax.fori_loop(0, qi, group, 0)
    for j in range(r):
        if j + 1 < r:
            qk(n_full + j + 1, (j + 1) % 2, (j + 1) * tk)
        softmax_pv(n_full + j, j % 2, j)
    acc = acc_ref[...]
    o = acc[0:V_HEAD, :] / acc[V_HEAD:V_HEAD + 1, :]
    o_ref[...] = o.T.astype(o_ref.dtype)


def _attention(q, k, vt, tq, tk):
    b, h, s, _ = q.shape
    assert (tq // tk) % 2 == 0, "key chunks are processed in pairs"
    kern = functools.partial(_attn_kernel, tq=tq, tk=tk)
    return pl.pallas_call(
        kern,
        grid=(b, h, s // tq),
        in_specs=[
            pl.BlockSpec((1, 1, tq, HEAD_PAD), lambda bi, hi, qi: (bi, hi, qi, 0)),
            pl.BlockSpec((1, 1, s, HEAD_PAD), lambda bi, hi, qi: (bi, hi, 0, 0)),
            pl.BlockSpec((1, 1, s // tk, VT_ROWS, tk), lambda bi, hi, qi: (bi, hi, 0, 0, 0)),
        ],
        out_specs=pl.BlockSpec((tq, V_HEAD), lambda bi, hi, qi: (bi * (s // tq) + qi, hi)),
        out_shape=jax.ShapeDtypeStruct((b * s, h * V_HEAD), BF16),
        scratch_shapes=[pltpu.VMEM((VT_ROWS, tq), F32), pltpu.VMEM((SUBLANES, tq), F32),
                        pltpu.VMEM((2, tk, tq), F32)],
        compiler_params=_cparams(("parallel", "parallel", "arbitrary")),
        name="mla_attention",
    )(q, k, vt)


def _out_proj_kernel(a_ref, w_ref, b_ref, x_ref, gpost_ref, gnext_ref, x1_ref, xn_ref):
    h = jnp.dot(a_ref[...], w_ref[...], preferred_element_type=F32) + b_ref[...]
    x1 = x_ref[...] + _rms(h, gpost_ref[...])
    x1_ref[...] = x1
    xn_ref[...] = _rms(x1, gnext_ref[...]).astype(xn_ref.dtype)


def _out_proj(a2d, w_stack, layer, bias, x2d, gpost, gnext, tm):
    m, kdim = a2d.shape
    d = w_stack.shape[-1]
    return pl.pallas_call(
        _out_proj_kernel,
        grid=(m // tm,),
        in_specs=[
            pl.BlockSpec((tm, kdim), lambda i: (i, 0)),
            _layer(w_stack.shape, layer), _resident((1, d)),
            pl.BlockSpec((tm, d), lambda i: (i, 0)),
            _resident((1, d)), _resident((1, d)),
        ],
        out_specs=[pl.BlockSpec((tm, d), lambda i: (i, 0)),
                   pl.BlockSpec((tm, d), lambda i: (i, 0))],
        out_shape=[jax.ShapeDtypeStruct((m, d), F32), jax.ShapeDtypeStruct((m, d), BF16)],
        compiler_params=_cparams(("parallel",)),
        name="out_proj",
    )(a2d, w_stack, bias, x2d, gpost, gnext)


def _mlp_kernel(xn_ref, w1_ref, w2_ref, x_ref, gpost_ref, gnext_ref, x2_ref, *rest):
    acc_ref = rest[-1]
    f = pl.program_id(1)

    @pl.when(f == 0)
    def _():
        acc_ref[...] = jnp.zeros(acc_ref.shape, F32)

    a = jnp.dot(xn_ref[...], w1_ref[...], preferred_element_type=F32)
    a = jnp.maximum(a, 0.0)
    a = (a * a).astype(BF16)
    acc_ref[...] += jnp.dot(a, w2_ref[...], preferred_element_type=F32)

    @pl.when(f == pl.num_programs(1) - 1)
    def _():
        x2 = x_ref[...] + _rms(acc_ref[...], gpost_ref[...])
        x2_ref[...] = x2
        if len(rest) == 2:
            rest[0][...] = _rms(x2, gnext_ref[...]).astype(rest[0].dtype)


def _mlp(xn2d, w1_stack, w2_stack, layer, x2d, gpost, gnext, emit_next, tm, tf):
    m, d = xn2d.shape
    ff = w1_stack.shape[-1]
    row_spec = pl.BlockSpec((tm, d), lambda i, f: (i, 0))
    out_specs = [row_spec] + ([row_spec] if emit_next else [])
    out_shape = [jax.ShapeDtypeStruct((m, d), F32)] + (
        [jax.ShapeDtypeStruct((m, d), BF16)] if emit_next else [])
    return pl.pallas_call(
        _mlp_kernel,
        grid=(m // tm, ff // tf),
        in_specs=[
            row_spec,
            pl.BlockSpec((None, d, tf), lambda i, f: (layer, 0, f)),
            pl.BlockSpec((None, tf, d), lambda i, f: (layer, f, 0)),
            row_spec,
            _resident((1, d)), _resident((1, d)),
        ],
        out_specs=out_specs,
        out_shape=out_shape,
        scratch_shapes=[pltpu.VMEM((tm, d), F32)],
        compiler_params=_cparams(("parallel", "arbitrary")),
        name="sq_relu_mlp",
    )(xn2d, w1_stack, w2_stack, x2d, gpost, gnext)


def _lru_in_kernel(xn_ref, wy_ref, by_ref, wx_ref, bx_ref, y_ref, u_ref):
    xn = xn_ref[...]
    zy = jnp.dot(xn, wy_ref[...], preferred_element_type=F32) + by_ref[...]
    y_ref[...] = _gelu_tanh(zy).astype(y_ref.dtype)
    u_ref[...] = jnp.dot(xn, wx_ref[...], preferred_element_type=F32) + bx_ref[...]


def _lru_in(xn2d, wy_stack, by, wx_stack, bx, layer, tm):
    m, d = xn2d.shape
    c = wy_stack.shape[-1]
    return pl.pallas_call(
        _lru_in_kernel,
        grid=(m // tm,),
        in_specs=[pl.BlockSpec((tm, d), lambda i: (i, 0)),
                  _layer(wy_stack.shape, layer), _resident((1, c)),
                  _layer(wx_stack.shape, layer), _resident((1, c))],
        out_specs=[pl.BlockSpec((tm, c), lambda i: (i, 0)),
                   pl.BlockSpec((tm, c), lambda i: (i, 0))],
        out_shape=[jax.ShapeDtypeStruct((m, c), BF16), jax.ShapeDtypeStruct((m, c), F32)],
        compiler_params=_cparams(("parallel",)),
        name="lru_in_proj",
    )(xn2d, wy_stack, by, wx_stack, bx)


GATE_TILE = 3 * LANES


def _gate_windows(c, nblk):
    blk = c // nblk
    wins = []
    for j in range(c // GATE_TILE):
        c0, c1 = j * GATE_TILE, (j + 1) * GATE_TILE
        n0, n1 = c0 // blk, (c1 - 1) // blk
        lo = (n0 * blk) // LANES * LANES
        hi = -(-((n1 + 1) * blk) // LANES) * LANES
        wins.append((lo, hi))
    return tuple(wins)


def _lru_core_kernel(*refs, wins, ts):
    nw = len(wins)
    (u_ref, y_ref, cw_ref, cb_ref, bga_ref, bgi_ref, lam_ref) = refs[:7]
    wg_refs = refs[7:7 + nw]
    g_ref = refs[7 + nw]
    ext_s, a_s, b_s, hc_s = refs[8 + nw:]
    c = u_ref.shape[-1]
    t = pl.program_id(1)

    @pl.when(t == 0)
    def _():
        ext_s[0:SUBLANES, :] = jnp.zeros((SUBLANES, c), F32)
        hc_s[...] = jnp.zeros(hc_s.shape, F32)

    up = u_ref[...]
    ext_s[SUBLANES:SUBLANES + ts, :] = up
    cw = cw_ref[...]
    u = cb_ref[...] + cw[CONV_WIDTH - 1:CONV_WIDTH, :] * up
    for kk in range(1, CONV_WIDTH):
        u = u + cw[CONV_WIDTH - 1 - kk:CONV_WIDTH - kk, :] * ext_s[pl.ds(SUBLANES - kk, ts), :]
    ext_s[0:SUBLANES, :] = ext_s[ts:ts + SUBLANES, :]

    ub = u.astype(BF16)
    za, zi = [], []
    for j, (lo, hi) in enumerate(wins):
        z = jnp.dot(ub[:, lo:hi], wg_refs[j][...], preferred_element_type=F32)
        za.append(z[:, :GATE_TILE])
        zi.append(z[:, GATE_TILE:])
    r = _sigmoid(jnp.concatenate(za, axis=1) + bga_ref[...])
    gi = _sigmoid(jnp.concatenate(zi, axis=1) + bgi_ref[...])
    lam = lam_ref[...]
    log_sig = jnp.minimum(lam, 0.0) - jnp.log1p(jnp.exp(-jnp.abs(lam)))
    log_a = r * (LRU_C * log_sig)
    a = jnp.exp(log_a)
    a_s[...] = a
    z = -jnp.tanh(log_a) * (a * a + 1.0)
    root = jnp.where(z > 0.0, z * lax.rsqrt(z), 0.0)
    b_s[...] = root * (gi * u)

    row = lax.broadcasted_iota(jnp.int32, (SUBLANES, c), 0)

    def body(g, hprev):
        r0 = pl.multiple_of(g * SUBLANES, SUBLANES)
        av = a_s[pl.ds(r0, SUBLANES), :]
        bv = b_s[pl.ds(r0, SUBLANES), :]
        for dd in (1, 2, 4):
            ash = pltpu.roll(av, dd, axis=0)
            bsh = pltpu.roll(bv, dd, axis=0)
            keep = row >= dd
            bv = jnp.where(keep, av * bsh + bv, bv)
            av = jnp.where(keep, av * ash, av)
        hv = av * hprev + bv
        b_s[pl.ds(r0, SUBLANES), :] = hv
        return hv[SUBLANES - 1:SUBLANES, :]

    hlast = lax.fori_loop(0, ts // SUBLANES, body, hc_s[0:1, :])
    hc_s[0:1, :] = hlast
    g_ref[...] = (b_s[...] * y_ref[...].astype(F32)).astype(g_ref.dtype)


def _lru_core(u_pre, y, b, cw, cb, bga, bgi, lam, wgs, wins, ts):
    m, c = u_pre.shape
    s = m // b
    nt = s // ts
    kern = functools.partial(_lru_core_kernel, wins=wins, ts=ts)
    row_spec = pl.BlockSpec((ts, c), lambda bi, t: (bi * nt + t, 0))
    return pl.pallas_call(
        kern,
        grid=(b, s // ts),
        in_specs=[row_spec, row_spec, _resident(cw.shape), _resident((1, c)),
                  _resident((1, c)), _resident((1, c)), _resident((1, c))]
                 + [_resident(w.shape) for w in wgs],
        out_specs=row_spec,
        out_shape=jax.ShapeDtypeStruct((m, c), BF16),
        scratch_shapes=[pltpu.VMEM((ts + SUBLANES, c), F32), pltpu.VMEM((ts, c), F32),
                        pltpu.VMEM((ts, c), F32), pltpu.VMEM((SUBLANES, c), F32)],
        compiler_params=_cparams(("parallel", "arbitrary")),
        name="lru_core",
    )(u_pre, y, cw, cb, bga, bgi, lam, *wgs)


def _prep_mla(w_dq, w_uq, w_dkv, w_ukv):
    d = w_dq.shape[0]
    q_lora = w_dq.shape[1]
    kv_lora = w_dkv.shape[1] - QK_ROPE
    half = QK_ROPE // 2
    zpad = jnp.zeros((d, LANES - QK_ROPE), w_dkv.dtype)
    kr = w_dkv[:, kv_lora:]
    kr_rot = jnp.concatenate([-kr[:, half:], kr[:, :half]], axis=1)
    wa = jnp.concatenate([w_dq, w_dkv[:, :kv_lora], kr, zpad, kr_rot, zpad], axis=1).astype(BF16)

    wq = w_uq.reshape(q_lora, MLA_HEADS, QK_HEAD)
    zq = jnp.zeros((q_lora, MLA_HEADS, LANES - QK_ROPE), w_uq.dtype)
    qr = wq[:, :, QK_NOPE:]
    qr_rot = jnp.concatenate([-qr[:, :, half:], qr[:, :, :half]], axis=2)
    wqn = wq[:, :, :QK_NOPE].reshape(q_lora, MLA_HEADS * LANES).astype(BF16)
    wqr = jnp.concatenate([qr, zq], axis=2).reshape(q_lora, MLA_HEADS * LANES).astype(BF16)
    wqrr = jnp.concatenate([qr_rot, zq], axis=2).reshape(q_lora, MLA_HEADS * LANES).astype(BF16)

    wkv = w_ukv.reshape(kv_lora, MLA_HEADS, QK_NOPE + V_HEAD)
    wkn = wkv[:, :, :QK_NOPE].reshape(kv_lora, MLA_HEADS * QK_NOPE).astype(BF16)
    wvt = wkv[:, :, QK_NOPE:].reshape(kv_lora, MLA_HEADS * V_HEAD).T.astype(BF16)
    return wa, wqn, wqr, wqrr, wkn, wvt


def _prep_gates(w_ga, w_gi, wins):
    nblk, blk, _ = w_ga.shape
    c = nblk * blk
    eye = jnp.eye(nblk, dtype=w_ga.dtype)

    def dense(w):
        return (eye[:, None, :, None] * w[:, :, None, :]).reshape(c, c)

    da, di = dense(w_ga), dense(w_gi)
    out = []
    for j, (lo, hi) in enumerate(wins):
        cols = slice(j * GATE_TILE, (j + 1) * GATE_TILE)
        out.append(jnp.concatenate([da[lo:hi, cols], di[lo:hi, cols]], axis=1).astype(BF16))
    return out


def _pick(n, pref):
    t = min(n, pref)
    while n % t:
        t //= 2
    return t


def kernel(x, positions, mix_pre_g, mix_post_g, mlp_pre_g, mlp_post_g, mla_w_dq, mla_g_q, mla_w_uq, mla_w_dkv, mla_g_kv, mla_w_ukv, mla_w_o, lru_w_y, lru_b_y, lru_w_x, lru_b_x, lru_conv_w, lru_conv_b, lru_w_ga, lru_b_ga, lru_w_gi, lru_b_gi, lru_lam, lru_w_out, lru_b_out, mlp_w1, mlp_w2):
    b, s, d = x.shape
    depth = mix_pre_g.shape[0]
    m = b * s
    c = lru_w_y.shape[-1]
    tm = _pick(m, 512)
    tq = _pick(s, 2048)
    tk = tq // 4
    ts = _pick(s, 256)
    tf = _pick(mlp_w1.shape[-1], 1024)
    wins = _gate_windows(c, LRU_BLOCKS)

    half = QK_ROPE // 2
    inv_freq = ROPE_THETA ** (-jnp.arange(half, dtype=F32) / half)
    invf = jnp.tile(inv_freq, LANES // half)[None, :]
    pos2d = positions.reshape(m, 1)
    zero_bias = jnp.zeros((1, d), F32)
    w1_b, w2_b = mlp_w1.astype(BF16), mlp_w2.astype(BF16)
    wo_b = mla_w_o.astype(BF16)
    wy_b, wx_b, wout_b = lru_w_y.astype(BF16), lru_w_x.astype(BF16), lru_w_out.astype(BF16)

    def row(v):
        return v[None, :].astype(F32)

    x2d = x.reshape(m, d)
    xn = _prenorm(x2d, row(mix_pre_g[0]), tm)
    for i in range(depth):
        j = i // N_MIXERS
        if i % N_MIXERS == 0:
            wa, wqn, wqr, wqrr, wkn, wvt = _prep_mla(
                mla_w_dq[j], mla_w_uq[j], mla_w_dkv[j], mla_w_ukv[j])
            q, k, vt = _mla_proj(xn, pos2d, b, invf, wa, row(mla_g_q[j]), row(mla_g_kv[j]),
                                 wqn, wqr, wqrr, wkn, wvt, tk)
            o = _attention(q, k, vt, tq, tk)
            x2d, xn = _out_proj(o, wo_b, j, zero_bias, x2d,
                                row(mix_post_g[i]), row(mlp_pre_g[i]), tm)
        else:
            y, u_pre = _lru_in(xn, wy_b, row(lru_b_y[j]), wx_b, row(lru_b_x[j]), j, tm)
            wgs = _prep_gates(lru_w_ga[j], lru_w_gi[j], wins)
            g = _lru_core(u_pre, y, b, lru_conv_w[j], row(lru_conv_b[j]), row(lru_b_ga[j]),
                          row(lru_b_gi[j]), row(lru_lam[j]), wgs, wins, ts)
            x2d, xn = _out_proj(g, wout_b, j, row(lru_b_out[j]), x2d,
                                row(mix_post_g[i]), row(mlp_pre_g[i]), tm)
        last = i + 1 == depth
        g_next = mix_pre_g[0] if last else mix_pre_g[i + 1]
        outs = _mlp(xn, w1_b, w2_b, i, x2d, row(mlp_post_g[i]), row(g_next), not last, tm, tf)
        x2d = outs[0]
        xn = None if last else outs[1]
    return x2d.reshape(b, s, d)
```

```python
import functools
import math

import jax
import jax.numpy as jnp
from jax import lax
from jax.experimental import pallas as pl
from jax.experimental.pallas import tpu as pltpu

MLA_HEADS = 16
QK_NOPE = 128
QK_ROPE = 64
V_HEAD = 128
QK_HEAD = QK_NOPE + QK_ROPE
ROPE_THETA = 10000.0
CHUNK = 64
LRU_BLOCKS = 16
CONV_WIDTH = 4
LRU_C = 8.0
EPS = 1e-6
N_MIXERS = 2

LANES = 128
SUBLANES = 8
BF16_SUBLANES = 16
V7X_SCOPED_VMEM_BYTES = 60000 * 1024

HEAD_PAD = 2 * LANES
VT_ROWS = V_HEAD + BF16_SUBLANES
NEG_BIG = -1e30

F32 = jnp.float32
BF16 = jnp.bfloat16


def _cparams(semantics):
    return pltpu.CompilerParams(dimension_semantics=semantics,
                                vmem_limit_bytes=V7X_SCOPED_VMEM_BYTES)


def _resident(shape):
    nd = len(shape)
    return pl.BlockSpec(shape, lambda *_: (0,) * nd, pipeline_mode=pl.Buffered(1))


def _layer(shape, layer):
    nd = len(shape) - 1
    return pl.BlockSpec((None,) + tuple(shape[1:]), lambda *_: (layer,) + (0,) * nd,
                        pipeline_mode=pl.Buffered(1))


def _rms(x, g):
    y = x * lax.rsqrt(jnp.mean(x * x, axis=-1, keepdims=True) + EPS)
    return y * g


def _sigmoid(x):
    return 0.5 * jnp.tanh(0.5 * x) + 0.5


def _gelu_tanh(x):
    c = math.sqrt(2.0 / math.pi)
    return 0.5 * x * (1.0 + jnp.tanh(c * (x + 0.044715 * (x * x * x))))


def _prenorm_kernel(x_ref, g_ref, o_ref):
    o_ref[...] = _rms(x_ref[...], g_ref[...]).astype(o_ref.dtype)


def _prenorm(x2d, g, tm):
    m, d = x2d.shape
    return pl.pallas_call(
        _prenorm_kernel,
        grid=(m // tm,),
        in_specs=[pl.BlockSpec((tm, d), lambda i: (i, 0)), _resident((1, d))],
        out_specs=pl.BlockSpec((tm, d), lambda i: (i, 0)),
        out_shape=jax.ShapeDtypeStruct((m, d), BF16),
        compiler_params=_cparams(("parallel",)),
        name="prenorm",
    )(x2d, g)


def _mla_proj_kernel(xn_ref, pos_ref, invf_ref, wa_ref, gq_ref, gkv_ref,
                     wqn_ref, wqr_ref, wqrr_ref, wkn_ref, wvt_ref,
                     q_ref, k_ref, vt_ref, *, qscale, q_lora, kv_lora, head_group):
    xn = xn_ref[...]
    a = jnp.dot(xn, wa_ref[...], preferred_element_type=F32)
    cq = _rms(a[:, :q_lora], gq_ref[...]).astype(BF16)
    ckv = _rms(a[:, q_lora:q_lora + kv_lora], gkv_ref[...]).astype(BF16)
    ang = pos_ref[...].astype(F32) * invf_ref[...]
    cos = jnp.cos(ang)
    sin = jnp.sin(ang)
    r0 = q_lora + kv_lora
    kr = (a[:, r0:r0 + LANES] * cos + a[:, r0 + LANES:r0 + 2 * LANES] * sin).astype(BF16)
    cs = cos * qscale
    sn = sin * qscale
    ones = jnp.ones((VT_ROWS - V_HEAD, vt_ref.shape[-1]), BF16)
    gw = head_group * LANES
    for g in range(MLA_HEADS // head_group):
        cols = slice(g * gw, (g + 1) * gw)
        qn = jnp.dot(cq, wqn_ref[:, cols], preferred_element_type=F32)
        qr = jnp.dot(cq, wqr_ref[:, cols], preferred_element_type=F32)
        qrr = jnp.dot(cq, wqrr_ref[:, cols], preferred_element_type=F32)
        kn = jnp.dot(ckv, wkn_ref[:, cols], preferred_element_type=F32)
        vt = lax.dot_general(wvt_ref[cols, :], ckv, (((1,), (1,)), ((), ())),
                             preferred_element_type=F32)
        for j in range(head_group):
            h = g * head_group + j
            sl = slice(j * LANES, (j + 1) * LANES)
            q_ref[0, h, :, 0:LANES] = (qn[:, sl] * qscale).astype(BF16)
            q_ref[0, h, :, LANES:HEAD_PAD] = (qr[:, sl] * cs + qrr[:, sl] * sn).astype(BF16)
            k_ref[0, h, :, 0:LANES] = kn[:, sl].astype(BF16)
            k_ref[0, h, :, LANES:HEAD_PAD] = kr
            vt_ref[0, h, 0, 0:V_HEAD, :] = vt[sl, :].astype(BF16)
            vt_ref[0, h, 0, V_HEAD:VT_ROWS, :] = ones


def _mla_proj(xn2d, pos2d, b, invf, wa, gq, gkv, wqn, wqr, wqrr, wkn, wvt, tk):
    m, d = xn2d.shape
    s = m // b
    nt = s // tk
    q_lora = gq.shape[-1]
    kv_lora = gkv.shape[-1]
    qscale = (QK_HEAD ** -0.5) * math.log2(math.e)
    kern = functools.partial(_mla_proj_kernel, qscale=qscale, q_lora=q_lora,
                             kv_lora=kv_lora, head_group=4)
    hq = MLA_HEADS
    return pl.pallas_call(
        kern,
        grid=(b, s // tk),
        in_specs=[
            pl.BlockSpec((tk, d), lambda bi, i: (bi * nt + i, 0)),
            pl.BlockSpec((tk, 1), lambda bi, i: (bi * nt + i, 0)),
            _resident(invf.shape), _resident(wa.shape), _resident(gq.shape),
            _resident(gkv.shape), _resident(wqn.shape), _resident(wqr.shape),
            _resident(wqrr.shape), _resident(wkn.shape), _resident(wvt.shape),
        ],
        out_specs=[
            pl.BlockSpec((1, hq, tk, HEAD_PAD), lambda bi, i: (bi, 0, i, 0)),
            pl.BlockSpec((1, hq, tk, HEAD_PAD), lambda bi, i: (bi, 0, i, 0)),
            pl.BlockSpec((1, hq, 1, VT_ROWS, tk), lambda bi, i: (bi, 0, i, 0, 0)),
        ],
        out_shape=[
            jax.ShapeDtypeStruct((b, hq, s, HEAD_PAD), BF16),
            jax.ShapeDtypeStruct((b, hq, s, HEAD_PAD), BF16),
            jax.ShapeDtypeStruct((b, hq, s // tk, VT_ROWS, tk), BF16),
        ],
        compiler_params=_cparams(("parallel", "parallel")),
        name="mla_proj",
    )(xn2d, pos2d, invf, wa, gq, gkv, wqn, wqr, wqrr, wkn, wvt)


def _attn_kernel(q_ref, k_ref, vt_ref, o_ref, acc_ref, m_ref, s_ref, mx_ref, *, tq, tk, ct):
    qi = pl.program_id(2)
    r = tq // tk
    acc_ref[...] = jnp.zeros(acc_ref.shape, F32)
    m_ref[...] = jnp.full(m_ref.shape, NEG_BIG, F32)
    n_full = qi * r

    def qk(ki, slot, c0):
        k0 = pl.multiple_of(ki * tk, tk)
        kb = k_ref[0, 0, pl.ds(k0, tk), :]
        for c in range(c0, tq, ct):
            s = lax.dot_general(kb, q_ref[0, 0, c:c + ct, :], (((1,), (1,)), ((), ())),
                                preferred_element_type=F32)
            s_ref[slot, :, c:c + ct] = s
            mx_ref[slot, 0:1, c:c + ct] = jnp.max(s, axis=0, keepdims=True)

    def softmax_pv(ki, slot, j):
        c0 = 0 if j is None else j * tk
        vt = vt_ref[0, 0, ki]
        for c in range(c0, tq, ct):
            s = s_ref[slot, :, c:c + ct]
            if j is None or c >= c0 + tk:
                m_chunk = mx_ref[slot, 0:1, c:c + ct]
            else:
                kc = (j * tk + lax.broadcasted_iota(jnp.int32, s.shape, 0)) // CHUNK
                qc = (c + lax.broadcasted_iota(jnp.int32, s.shape, 1)) // CHUNK
                s = jnp.where(kc <= qc, s, NEG_BIG)
                m_chunk = jnp.max(s, axis=0, keepdims=True)
            m_old = m_ref[0:1, c:c + ct]
            m_new = jnp.maximum(m_old, m_chunk)
            alpha = jnp.exp2(m_old - m_new)
            p = jnp.exp2(s - m_new).astype(BF16)
            pv = jnp.dot(vt, p, preferred_element_type=F32)
            acc_ref[:, c:c + ct] = acc_ref[:, c:c + ct] * alpha + pv
            m_ref[0:1, c:c + ct] = m_new

    qk(0, 0, 0)

    def group(g, carry):
        for u in range(r):
            qk(g * r + u + 1, (u + 1) % 2, 0)
            softmax_pv(g * r + u, u % 2, None)
        return carry

    lax.fori_loop(0, qi, group, 0)
    for j in range(r):
        if j + 1 < r:
            qk(n_full + j + 1, (j + 1) % 2, (j + 1) * tk)
        softmax_pv(n_full + j, j % 2, j)
    acc = acc_ref[...]
    o = acc[0:V_HEAD, :] / acc[V_HEAD:V_HEAD + 1, :]
    o_ref[...] = o.T.astype(o_ref.dtype)


def _attention(q, k, vt, tq, tk):
    b, h, s, _ = q.shape
    assert (tq // tk) % 2 == 0, "key chunks are processed in pairs"
    kern = functools.partial(_attn_kernel, tq=tq, tk=tk, ct=tk)
    return pl.pallas_call(
        kern,
        grid=(b, h, s // tq),
        in_specs=[
            pl.BlockSpec((1, 1, tq, HEAD_PAD), lambda bi, hi, qi: (bi, hi, qi, 0)),
            pl.BlockSpec((1, 1, s, HEAD_PAD), lambda bi, hi, qi: (bi, hi, 0, 0)),
            pl.BlockSpec((1, 1, s // tk, VT_ROWS, tk), lambda bi, hi, qi: (bi, hi, 0, 0, 0)),
        ],
        out_specs=pl.BlockSpec((tq, V_HEAD), lambda bi, hi, qi: (bi * (s // tq) + qi, hi)),
        out_shape=jax.ShapeDtypeStruct((b * s, h * V_HEAD), BF16),
        scratch_shapes=[pltpu.VMEM((VT_ROWS, tq), F32), pltpu.VMEM((SUBLANES, tq), F32),
                        pltpu.VMEM((2, tk, tq), F32), pltpu.VMEM((2, SUBLANES, tq), F32)],
        compiler_params=_cparams(("parallel", "parallel", "arbitrary")),
        name="mla_attention",
    )(q, k, vt)


def _out_proj_kernel(a_ref, w_ref, b_ref, x_ref, gpost_ref, gnext_ref, x1_ref, xn_ref, *, parts):
    rows = a_ref.shape[0] // parts
    for p in range(parts):
        rs = slice(p * rows, (p + 1) * rows)
        h = jnp.dot(a_ref[rs, :], w_ref[...], preferred_element_type=F32) + b_ref[...]
        x1 = x_ref[rs, :] + _rms(h, gpost_ref[...])
        x1_ref[rs, :] = x1
        xn_ref[rs, :] = _rms(x1, gnext_ref[...]).astype(xn_ref.dtype)


def _out_proj(a2d, w_stack, layer, bias, x2d, gpost, gnext, tm):
    m, kdim = a2d.shape
    d = w_stack.shape[-1]
    return pl.pallas_call(
        functools.partial(_out_proj_kernel, parts=4),
        grid=(m // tm,),
        in_specs=[
            pl.BlockSpec((tm, kdim), lambda i: (i, 0)),
            _layer(w_stack.shape, layer), _resident((1, d)),
            pl.BlockSpec((tm, d), lambda i: (i, 0)),
            _resident((1, d)), _resident((1, d)),
        ],
        out_specs=[pl.BlockSpec((tm, d), lambda i: (i, 0)),
                   pl.BlockSpec((tm, d), lambda i: (i, 0))],
        out_shape=[jax.ShapeDtypeStruct((m, d), F32), jax.ShapeDtypeStruct((m, d), BF16)],
        compiler_params=_cparams(("parallel",)),
        name="out_proj",
    )(a2d, w_stack, bias, x2d, gpost, gnext)


def _mlp_kernel(xn_ref, w1_ref, w2_ref, x_ref, gpost_ref, gnext_ref, x2_ref, *rest):
    acc_ref = rest[-1]
    f = pl.program_id(1)

    @pl.when(f == 0)
    def _():
        acc_ref[...] = jnp.zeros(acc_ref.shape, F32)

    a = jnp.dot(xn_ref[...], w1_ref[...], preferred_element_type=F32)
    a = jnp.maximum(a, 0.0)
    a = (a * a).astype(BF16)
    acc_ref[...] += jnp.dot(a, w2_ref[...], preferred_element_type=F32)

    @pl.when(f == pl.num_programs(1) - 1)
    def _():
        x2 = x_ref[...] + _rms(acc_ref[...], gpost_ref[...])
        x2_ref[...] = x2
        if len(rest) == 2:
            rest[0][...] = _rms(x2, gnext_ref[...]).astype(rest[0].dtype)


def _mlp(xn2d, w1_stack, w2_stack, layer, x2d, gpost, gnext, emit_next, tm, tf):
    m, d = xn2d.shape
    ff = w1_stack.shape[-1]
    row_spec = pl.BlockSpec((tm, d), lambda i, f: (i, 0))
    out_specs = [row_spec] + ([row_spec] if emit_next else [])
    out_shape = [jax.ShapeDtypeStruct((m, d), F32)] + (
        [jax.ShapeDtypeStruct((m, d), BF16)] if emit_next else [])
    return pl.pallas_call(
        _mlp_kernel,
        grid=(m // tm, ff // tf),
        in_specs=[
            row_spec,
            pl.BlockSpec((None, d, tf), lambda i, f: (layer, 0, f)),
            pl.BlockSpec((None, tf, d), lambda i, f: (layer, f, 0)),
            row_spec,
            _resident((1, d)), _resident((1, d)),
        ],
        out_specs=out_specs,
        out_shape=out_shape,
        scratch_shapes=[pltpu.VMEM((tm, d), F32)],
        compiler_params=_cparams(("parallel", "arbitrary")),
        name="sq_relu_mlp",
    )(xn2d, w1_stack, w2_stack, x2d, gpost, gnext)


def _lru_in_kernel(xn_ref, wy_ref, by_ref, wx_ref, bx_ref, y_ref, u_ref):
    xn = xn_ref[...]
    zy = jnp.dot(xn, wy_ref[...], preferred_element_type=F32) + by_ref[...]
    y_ref[...] = _gelu_tanh(zy).astype(y_ref.dtype)
    u_ref[...] = jnp.dot(xn, wx_ref[...], preferred_element_type=F32) + bx_ref[...]


def _lru_in(xn2d, wy_stack, by, wx_stack, bx, layer, tm):
    m, d = xn2d.shape
    c = wy_stack.shape[-1]
    return pl.pallas_call(
        _lru_in_kernel,
        grid=(m // tm,),
        in_specs=[pl.BlockSpec((tm, d), lambda i: (i, 0)),
                  _layer(wy_stack.shape, layer), _resident((1, c)),
                  _layer(wx_stack.shape, layer), _resident((1, c))],
        out_specs=[pl.BlockSpec((tm, c), lambda i: (i, 0)),
                   pl.BlockSpec((tm, c), lambda i: (i, 0))],
        out_shape=[jax.ShapeDtypeStruct((m, c), BF16), jax.ShapeDtypeStruct((m, c), F32)],
        compiler_params=_cparams(("parallel",)),
        name="lru_in_proj",
    )(xn2d, wy_stack, by, wx_stack, bx)


GATE_TILE = 3 * LANES


def _gate_windows(c, nblk):
    blk = c // nblk
    wins = []
    for j in range(c // GATE_TILE):
        c0, c1 = j * GATE_TILE, (j + 1) * GATE_TILE
        n0, n1 = c0 // blk, (c1 - 1) // blk
        lo = (n0 * blk) // LANES * LANES
        hi = -(-((n1 + 1) * blk) // LANES) * LANES
        wins.append((lo, hi))
    return tuple(wins)


def _lru_core_kernel(*refs, wins, ts, lane_chunks):
    nw = len(wins)
    (u_ref, y_ref, cw_ref, cb_ref, bga_ref, bgi_ref, lam_ref) = refs[:7]
    wg_refs = refs[7:7 + nw]
    g_ref = refs[7 + nw]
    pu_s, hist_s, a_s, b_s, hp_s, hc_s = refs[8 + nw:]
    c = u_ref.shape[-1]
    nl = c // LANES
    seg = ts // SUBLANES
    hist = CONV_WIDTH - 1
    t = pl.program_id(1)

    @pl.when(t == 0)
    def _():
        hist_s[...] = jnp.zeros(hist_s.shape, F32)
        hc_s[...] = jnp.zeros(hc_s.shape, F32)

    for l in range(nl):
        for i in range(SUBLANES):
            pu_s[l, pl.ds(i, seg, stride=SUBLANES), :] = (
                u_ref[i * seg:(i + 1) * seg, l * LANES:(l + 1) * LANES])
    up = jnp.concatenate([pu_s[l] for l in range(nl)], axis=1)

    row8 = lax.broadcasted_iota(jnp.int32, (SUBLANES, c), 0)
    lead = []
    for g in range(hist):
        lo = (seg - hist + g) * SUBLANES
        cur = up[lo:lo + SUBLANES, :]
        prev = hist_s[g * SUBLANES:(g + 1) * SUBLANES, :]
        lead.append(pltpu.roll(jnp.where(row8 == SUBLANES - 1, prev, cur), 1, axis=0))
    hist_s[...] = up[(seg - hist) * SUBLANES:, :]
    ext = jnp.concatenate(lead + [up], axis=0)
    cw = cw_ref[...]
    u = cb_ref[...] + cw[hist:hist + 1, :] * up
    for kk in range(1, CONV_WIDTH):
        lo = (hist - kk) * SUBLANES
        u = u + cw[hist - kk:hist - kk + 1, :] * ext[lo:lo + ts, :]

    ub = u.astype(BF16)
    za, zi = [], []
    for j, (lo, hi) in enumerate(wins):
        z = jnp.dot(ub[:, lo:hi], wg_refs[j][...], preferred_element_type=F32)
        za.append(z[:, :GATE_TILE])
        zi.append(z[:, GATE_TILE:])
    r = _sigmoid(jnp.concatenate(za, axis=1) + bga_ref[...])
    gi = _sigmoid(jnp.concatenate(zi, axis=1) + bgi_ref[...])
    lam = lam_ref[...]
    log_sig = jnp.minimum(lam, 0.0) - jnp.log1p(jnp.exp(-jnp.abs(lam)))
    log_a = r * (LRU_C * log_sig)
    a = jnp.exp(log_a)
    a_s[...] = a
    z = -jnp.tanh(log_a) * (a * a + 1.0)
    root = jnp.where(z > 0.0, z * lax.rsqrt(z), 0.0)
    b_s[...] = root * (gi * u)

    def group(ref, j, c0, c1):
        return ref[pl.ds(pl.multiple_of(j * SUBLANES, SUBLANES), SUBLANES), c0:c1]

    tot_a, tot_h = [], []
    for c0, c1 in lane_chunks:
        def totals(j, carry, c0=c0, c1=c1):
            pa, ph = carry
            av = group(a_s, j, c0, c1)
            return av * pa, av * ph + group(b_s, j, c0, c1)

        pa, ph = lax.fori_loop(0, seg, totals, (jnp.ones((SUBLANES, c1 - c0), F32),
                                                jnp.zeros((SUBLANES, c1 - c0), F32)), unroll=4)
        tot_a.append(pa)
        tot_h.append(ph)
    tot_a = jnp.concatenate(tot_a, axis=1)
    tot_h = jnp.concatenate(tot_h, axis=1)

    state = hc_s[0:1, :]
    h_in = jnp.zeros((SUBLANES, c), F32)
    for i in range(SUBLANES):
        h_in = jnp.where(row8 == i, state, h_in)
        state = tot_a[i:i + 1, :] * state + tot_h[i:i + 1, :]
    hc_s[0:1, :] = state

    for c0, c1 in lane_chunks:
        def recur(j, h, c0=c0, c1=c1):
            h = group(a_s, j, c0, c1) * h + group(b_s, j, c0, c1)
            r0 = pl.multiple_of(j * SUBLANES, SUBLANES)
            for l in range(c0 // LANES, c1 // LANES):
                hp_s[l, pl.ds(r0, SUBLANES), :] = h[:, l * LANES - c0:(l + 1) * LANES - c0]
            return h

        lax.fori_loop(0, seg, recur, h_in[:, c0:c1], unroll=4)

    for i in range(SUBLANES):
        h_seg = jnp.concatenate(
            [hp_s[l, pl.ds(i, seg, stride=SUBLANES), :] for l in range(nl)], axis=1)
        rows = slice(i * seg, (i + 1) * seg)
        g_ref[rows, :] = (h_seg * y_ref[rows, :].astype(F32)).astype(g_ref.dtype)


def _lru_core(u_pre, y, b, cw, cb, bga, bgi, lam, wgs, wins, ts):
    m, c = u_pre.shape
    s = m // b
    nt = s // ts
    nl = c // LANES
    step = max(1, nl // 3) * LANES
    lane_chunks = tuple((c0, min(c, c0 + step)) for c0 in range(0, c, step))
    kern = functools.partial(_lru_core_kernel, wins=wins, ts=ts, lane_chunks=lane_chunks)
    row_spec = pl.BlockSpec((ts, c), lambda bi, t: (bi * nt + t, 0))
    return pl.pallas_call(
        kern,
        grid=(b, s // ts),
        in_specs=[row_spec, row_spec, _resident(cw.shape), _resident((1, c)),
                  _resident((1, c)), _resident((1, c)), _resident((1, c))]
                 + [_resident(w.shape) for w in wgs],
        out_specs=row_spec,
        out_shape=jax.ShapeDtypeStruct((m, c), BF16),
        scratch_shapes=[pltpu.VMEM((nl, ts, LANES), F32),
                        pltpu.VMEM(((CONV_WIDTH - 1) * SUBLANES, c), F32),
                        pltpu.VMEM((ts, c), F32), pltpu.VMEM((ts, c), F32),
                        pltpu.VMEM((nl, ts, LANES), F32), pltpu.VMEM((SUBLANES, c), F32)],
        compiler_params=_cparams(("parallel", "arbitrary")),
        name="lru_core",
    )(u_pre, y, cw, cb, bga, bgi, lam, *wgs)


def _prep_mla(w_dq, w_uq, w_dkv, w_ukv):
    d = w_dq.shape[0]
    q_lora = w_dq.shape[1]
    kv_lora = w_dkv.shape[1] - QK_ROPE
    half = QK_ROPE // 2
    zpad = jnp.zeros((d, LANES - QK_ROPE), w_dkv.dtype)
    kr = w_dkv[:, kv_lora:]
    kr_rot = jnp.concatenate([-kr[:, half:], kr[:, :half]], axis=1)
    wa = jnp.concatenate([w_dq, w_dkv[:, :kv_lora], kr, zpad, kr_rot, zpad], axis=1).astype(BF16)

    wq = w_uq.reshape(q_lora, MLA_HEADS, QK_HEAD)
    zq = jnp.zeros((q_lora, MLA_HEADS, LANES - QK_ROPE), w_uq.dtype)
    qr = wq[:, :, QK_NOPE:]
    qr_rot = jnp.concatenate([-qr[:, :, half:], qr[:, :, :half]], axis=2)
    wqn = wq[:, :, :QK_NOPE].reshape(q_lora, MLA_HEADS * LANES).astype(BF16)
    wqr = jnp.concatenate([qr, zq], axis=2).reshape(q_lora, MLA_HEADS * LANES).astype(BF16)
    wqrr = jnp.concatenate([qr_rot, zq], axis=2).reshape(q_lora, MLA_HEADS * LANES).astype(BF16)

    wkv = w_ukv.reshape(kv_lora, MLA_HEADS, QK_NOPE + V_HEAD)
    wkn = wkv[:, :, :QK_NOPE].reshape(kv_lora, MLA_HEADS * QK_NOPE).astype(BF16)
    wvt = wkv[:, :, QK_NOPE:].reshape(kv_lora, MLA_HEADS * V_HEAD).T.astype(BF16)
    return wa, wqn, wqr, wqrr, wkn, wvt


def _prep_gates(w_ga, w_gi, wins):
    nblk, blk, _ = w_ga.shape
    c = nblk * blk
    eye = jnp.eye(nblk, dtype=w_ga.dtype)

    def dense(w):
        return (eye[:, None, :, None] * w[:, :, None, :]).reshape(c, c)

    da, di = dense(w_ga), dense(w_gi)
    out = []
    for j, (lo, hi) in enumerate(wins):
        cols = slice(j * GATE_TILE, (j + 1) * GATE_TILE)
        out.append(jnp.concatenate([da[lo:hi, cols], di[lo:hi, cols]], axis=1).astype(BF16))
    return out


def _pick(n, pref):
    t = min(n, pref)
    while n % t:
        t //= 2
    return t


def kernel(x, positions, mix_pre_g, mix_post_g, mlp_pre_g, mlp_post_g, mla_w_dq, mla_g_q, mla_w_uq, mla_w_dkv, mla_g_kv, mla_w_ukv, mla_w_o, lru_w_y, lru_b_y, lru_w_x, lru_b_x, lru_conv_w, lru_conv_b, lru_w_ga, lru_b_ga, lru_w_gi, lru_b_gi, lru_lam, lru_w_out, lru_b_out, mlp_w1, mlp_w2):
    b, s, d = x.shape
    depth = mix_pre_g.shape[0]
    m = b * s
    c = lru_w_y.shape[-1]
    tm = _pick(m, 512)
    tq = _pick(s, 2048)
    tk = tq // 4
    ts = _pick(s, 256)
    tf = _pick(mlp_w1.shape[-1], 1024)
    wins = _gate_windows(c, LRU_BLOCKS)

    half = QK_ROPE // 2
    inv_freq = ROPE_THETA ** (-jnp.arange(half, dtype=F32) / half)
    invf = jnp.tile(inv_freq, LANES // half)[None, :]
    pos2d = positions.reshape(m, 1)
    zero_bias = jnp.zeros((1, d), F32)
    w1_b, w2_b = mlp_w1.astype(BF16), mlp_w2.astype(BF16)
    wo_b = mla_w_o.astype(BF16)
    wy_b, wx_b, wout_b = lru_w_y.astype(BF16), lru_w_x.astype(BF16), lru_w_out.astype(BF16)

    def row(v):
        return v[None, :].astype(F32)

    x2d = x.reshape(m, d)
    xn = _prenorm(x2d, row(mix_pre_g[0]), tm)
    for i in range(depth):
        j = i // N_MIXERS
        if i % N_MIXERS == 0:
            wa, wqn, wqr, wqrr, wkn, wvt = _prep_mla(
                mla_w_dq[j], mla_w_uq[j], mla_w_dkv[j], mla_w_ukv[j])
            q, k, vt = _mla_proj(xn, pos2d, b, invf, wa, row(mla_g_q[j]), row(mla_g_kv[j]),
                                 wqn, wqr, wqrr, wkn, wvt, tk)
            o = _attention(q, k, vt, tq, tk)
            x2d, xn = _out_proj(o, wo_b, j, zero_bias, x2d,
                                row(mix_post_g[i]), row(mlp_pre_g[i]), tm)
        else:
            y, u_pre = _lru_in(xn, wy_b, row(lru_b_y[j]), wx_b, row(lru_b_x[j]), j, tm)
            wgs = _prep_gates(lru_w_ga[j], lru_w_gi[j], wins)
            g = _lru_core(u_pre, y, b, lru_conv_w[j], row(lru_conv_b[j]), row(lru_b_ga[j]),
                          row(lru_b_gi[j]), row(lru_lam[j]), wgs, wins, ts)
            x2d, xn = _out_proj(g, wout_b, j, row(lru_b_out[j]), x2d,
                                row(mix_post_g[i]), row(mlp_pre_g[i]), tm)
        last = i + 1 == depth
        g_next = mix_pre_g[0] if last else mix_pre_g[i + 1]
        outs = _mlp(xn, w1_b, w2_b, i, x2d, row(mlp_post_g[i]), row(g_next), not last, tm, tf)
        x2d = outs[0]
        xn = None if last else outs[1]
    return x2d.reshape(b, s, d)
```

```python
import functools
import math

import jax
import jax.numpy as jnp
from jax import lax
from jax.experimental import pallas as pl
from jax.experimental.pallas import tpu as pltpu

MLA_HEADS = 16
QK_NOPE = 128
QK_ROPE = 64
V_HEAD = 128
QK_HEAD = QK_NOPE + QK_ROPE
ROPE_THETA = 10000.0
CHUNK = 64
LRU_BLOCKS = 16
CONV_WIDTH = 4
LRU_C = 8.0
EPS = 1e-6
N_MIXERS = 2

LANES = 128
SUBLANES = 8
BF16_SUBLANES = 16
V7X_SCOPED_VMEM_BYTES = 60000 * 1024

HEAD_PAD = 2 * LANES
VT_ROWS = V_HEAD + BF16_SUBLANES
NEG_BIG = -1e30

F32 = jnp.float32
BF16 = jnp.bfloat16


def _cparams(semantics):
    return pltpu.CompilerParams(dimension_semantics=semantics,
                                vmem_limit_bytes=V7X_SCOPED_VMEM_BYTES)


def _resident(shape):
    nd = len(shape)
    return pl.BlockSpec(shape, lambda *_: (0,) * nd, pipeline_mode=pl.Buffered(1))


def _layer(shape, layer):
    nd = len(shape) - 1
    return pl.BlockSpec((None,) + tuple(shape[1:]), lambda *_: (layer,) + (0,) * nd,
                        pipeline_mode=pl.Buffered(1))


def _rms(x, g):
    y = x * lax.rsqrt(jnp.mean(x * x, axis=-1, keepdims=True) + EPS)
    return y * g


def _sigmoid(x):
    return 0.5 * jnp.tanh(0.5 * x) + 0.5


def _gelu_tanh(x):
    c = math.sqrt(2.0 / math.pi)
    return 0.5 * x * (1.0 + jnp.tanh(c * (x + 0.044715 * (x * x * x))))


def _prenorm_kernel(x_ref, g_ref, o_ref):
    o_ref[...] = _rms(x_ref[...], g_ref[...]).astype(o_ref.dtype)


def _prenorm(x2d, g, tm):
    m, d = x2d.shape
    return pl.pallas_call(
        _prenorm_kernel,
        grid=(m // tm,),
        in_specs=[pl.BlockSpec((tm, d), lambda i: (i, 0)), _resident((1, d))],
        out_specs=pl.BlockSpec((tm, d), lambda i: (i, 0)),
        out_shape=jax.ShapeDtypeStruct((m, d), BF16),
        compiler_params=_cparams(("parallel",)),
        name="prenorm",
    )(x2d, g)


def _mla_proj_kernel(xn_ref, pos_ref, invf_ref, wa_ref, gq_ref, gkv_ref,
                     wqn_ref, wqr_ref, wqrr_ref, wkn_ref, wvt_ref,
                     q_ref, k_ref, vt_ref, *, qscale, q_lora, kv_lora, head_group):
    xn = xn_ref[...]
    a = jnp.dot(xn, wa_ref[...], preferred_element_type=F32)
    cq = _rms(a[:, :q_lora], gq_ref[...]).astype(BF16)
    ckv = _rms(a[:, q_lora:q_lora + kv_lora], gkv_ref[...]).astype(BF16)
    ang = pos_ref[...].astype(F32) * invf_ref[...]
    cos = jnp.cos(ang)
    sin = jnp.sin(ang)
    r0 = q_lora + kv_lora
    kr = (a[:, r0:r0 + LANES] * cos + a[:, r0 + LANES:r0 + 2 * LANES] * sin).astype(BF16)
    cs = cos * qscale
    sn = sin * qscale
    ones = jnp.ones((VT_ROWS - V_HEAD, vt_ref.shape[-1]), BF16)
    gw = head_group * LANES
    for g in range(MLA_HEADS // head_group):
        cols = slice(g * gw, (g + 1) * gw)
        qn = jnp.dot(cq, wqn_ref[:, cols], preferred_element_type=F32)
        qr = jnp.dot(cq, wqr_ref[:, cols], preferred_element_type=F32)
        qrr = jnp.dot(cq, wqrr_ref[:, cols], preferred_element_type=F32)
        kn = jnp.dot(ckv, wkn_ref[:, cols], preferred_element_type=F32)
        vt = lax.dot_general(wvt_ref[cols, :], ckv, (((1,), (1,)), ((), ())),
                             preferred_element_type=F32)
        for j in range(head_group):
            h = g * head_group + j
            sl = slice(j * LANES, (j + 1) * LANES)
            q_ref[0, h, :, 0:LANES] = (qn[:, sl] * qscale).astype(BF16)
            q_ref[0, h, :, LANES:HEAD_PAD] = (qr[:, sl] * cs + qrr[:, sl] * sn).astype(BF16)
            k_ref[0, h, :, 0:LANES] = kn[:, sl].astype(BF16)
            k_ref[0, h, :, LANES:HEAD_PAD] = kr
            vt_ref[0, h, 0, 0:V_HEAD, :] = vt[sl, :].astype(BF16)
            vt_ref[0, h, 0, V_HEAD:VT_ROWS, :] = ones


def _mla_proj(xn2d, pos2d, b, invf, wa, gq, gkv, wqn, wqr, wqrr, wkn, wvt, tk):
    m, d = xn2d.shape
    s = m // b
    nt = s // tk
    q_lora = gq.shape[-1]
    kv_lora = gkv.shape[-1]
    qscale = (QK_HEAD ** -0.5) * math.log2(math.e)
    kern = functools.partial(_mla_proj_kernel, qscale=qscale, q_lora=q_lora,
                             kv_lora=kv_lora, head_group=4)
    hq = MLA_HEADS
    return pl.pallas_call(
        kern,
        grid=(b, s // tk),
        in_specs=[
            pl.BlockSpec((tk, d), lambda bi, i: (bi * nt + i, 0)),
            pl.BlockSpec((tk, 1), lambda bi, i: (bi * nt + i, 0)),
            _resident(invf.shape), _resident(wa.shape), _resident(gq.shape),
            _resident(gkv.shape), _resident(wqn.shape), _resident(wqr.shape),
            _resident(wqrr.shape), _resident(wkn.shape), _resident(wvt.shape),
        ],
        out_specs=[
            pl.BlockSpec((1, hq, tk, HEAD_PAD), lambda bi, i: (bi, 0, i, 0)),
            pl.BlockSpec((1, hq, tk, HEAD_PAD), lambda bi, i: (bi, 0, i, 0)),
            pl.BlockSpec((1, hq, 1, VT_ROWS, tk), lambda bi, i: (bi, 0, i, 0, 0)),
        ],
        out_shape=[
            jax.ShapeDtypeStruct((b, hq, s, HEAD_PAD), BF16),
            jax.ShapeDtypeStruct((b, hq, s, HEAD_PAD), BF16),
            jax.ShapeDtypeStruct((b, hq, s // tk, VT_ROWS, tk), BF16),
        ],
        compiler_params=_cparams(("parallel", "parallel")),
        name="mla_proj",
    )(xn2d, pos2d, invf, wa, gq, gkv, wqn, wqr, wqrr, wkn, wvt)


def _attn_kernel(q_ref, k_ref, vt_ref, o_ref, acc_ref, m_ref, s_ref, mx_ref, *, tq, tk, ct):
    qi = pl.program_id(2)
    r = tq // tk
    acc_ref[...] = jnp.zeros(acc_ref.shape, F32)
    m_ref[...] = jnp.full(m_ref.shape, NEG_BIG, F32)
    n_full = qi * r

    def qk(ki, slot, c0):
        k0 = pl.multiple_of(ki * tk, tk)
        kb = k_ref[0, 0, pl.ds(k0, tk), :]
        for c in range(c0, tq, ct):
            s = lax.dot_general(kb, q_ref[0, 0, c:c + ct, :], (((1,), (1,)), ((), ())),
                                preferred_element_type=F32)
            s_ref[slot, :, c:c + ct] = s
            mx_ref[slot, 0:1, c:c + ct] = jnp.max(s, axis=0, keepdims=True)

    def softmax_pv(ki, slot, j):
        c0 = 0 if j is None else j * tk
        vt = vt_ref[0, 0, ki]
        for c in range(c0, tq, ct):
            s = s_ref[slot, :, c:c + ct]
            if j is None or c >= c0 + tk:
                m_chunk = mx_ref[slot, 0:1, c:c + ct]
            else:
                kc = (j * tk + lax.broadcasted_iota(jnp.int32, s.shape, 0)) // CHUNK
                qc = (c + lax.broadcasted_iota(jnp.int32, s.shape, 1)) // CHUNK
                s = jnp.where(kc <= qc, s, NEG_BIG)
                m_chunk = jnp.max(s, axis=0, keepdims=True)
            m_old = m_ref[0:1, c:c + ct]
            m_new = jnp.maximum(m_old, m_chunk)
            alpha = jnp.exp2(m_old - m_new)
            p = jnp.exp2(s - m_new).astype(BF16)
            pv = jnp.dot(vt, p, preferred_element_type=F32)
            acc_ref[:, c:c + ct] = acc_ref[:, c:c + ct] * alpha + pv
            m_ref[0:1, c:c + ct] = m_new

    qk(0, 0, 0)

    def group(g, carry):
        for u in range(r):
            qk(g * r + u + 1, (u + 1) % 2, 0)
            softmax_pv(g * r + u, u % 2, None)
        return carry

    lax.fori_loop(0, qi, group, 0)
    for j in range(r):
        if j + 1 < r:
            qk(n_full + j + 1, (j + 1) % 2, (j + 1) * tk)
        softmax_pv(n_full + j, j % 2, j)
    acc = acc_ref[...]
    o = acc[0:V_HEAD, :] / acc[V_HEAD:V_HEAD + 1, :]
    o_ref[...] = o.T.astype(o_ref.dtype)


def _attention(q, k, vt, tq, tk):
    b, h, s, _ = q.shape
    assert (tq // tk) % 2 == 0, "key chunks are processed in pairs"
    kern = functools.partial(_attn_kernel, tq=tq, tk=tk, ct=tk)
    return pl.pallas_call(
        kern,
        grid=(b, h, s // tq),
        in_specs=[
            pl.BlockSpec((1, 1, tq, HEAD_PAD), lambda bi, hi, qi: (bi, hi, qi, 0)),
            pl.BlockSpec((1, 1, s, HEAD_PAD), lambda bi, hi, qi: (bi, hi, 0, 0)),
            pl.BlockSpec((1, 1, s // tk, VT_ROWS, tk), lambda bi, hi, qi: (bi, hi, 0, 0, 0)),
        ],
        out_specs=pl.BlockSpec((tq, V_HEAD), lambda bi, hi, qi: (bi * (s // tq) + qi, hi)),
        out_shape=jax.ShapeDtypeStruct((b * s, h * V_HEAD), BF16),
        scratch_shapes=[pltpu.VMEM((VT_ROWS, tq), F32), pltpu.VMEM((SUBLANES, tq), F32),
                        pltpu.VMEM((2, tk, tq), F32), pltpu.VMEM((2, SUBLANES, tq), F32)],
        compiler_params=_cparams(("parallel", "parallel", "arbitrary")),
        name="mla_attention",
    )(q, k, vt)


def _out_proj_kernel(a_ref, w_ref, b_ref, x_ref, gpost_ref, gnext_ref, x1_ref, xn_ref, *, parts):
    rows = a_ref.shape[0] // parts
    for p in range(parts):
        rs = slice(p * rows, (p + 1) * rows)
        h = jnp.dot(a_ref[rs, :], w_ref[...], preferred_element_type=F32) + b_ref[...]
        x1 = x_ref[rs, :] + _rms(h, gpost_ref[...])
        x1_ref[rs, :] = x1
        xn_ref[rs, :] = _rms(x1, gnext_ref[...]).astype(xn_ref.dtype)


def _out_proj(a2d, w_stack, layer, bias, x2d, gpost, gnext, tm):
    m, kdim = a2d.shape
    d = w_stack.shape[-1]
    return pl.pallas_call(
        functools.partial(_out_proj_kernel, parts=4),
        grid=(m // tm,),
        in_specs=[
            pl.BlockSpec((tm, kdim), lambda i: (i, 0)),
            _layer(w_stack.shape, layer), _resident((1, d)),
            pl.BlockSpec((tm, d), lambda i: (i, 0)),
            _resident((1, d)), _resident((1, d)),
        ],
        out_specs=[pl.BlockSpec((tm, d), lambda i: (i, 0)),
                   pl.BlockSpec((tm, d), lambda i: (i, 0))],
        out_shape=[jax.ShapeDtypeStruct((m, d), F32), jax.ShapeDtypeStruct((m, d), BF16)],
        compiler_params=_cparams(("parallel",)),
        name="out_proj",
    )(a2d, w_stack, bias, x2d, gpost, gnext)


def _mlp_kernel(xn_ref, w1_ref, w2_ref, x_ref, gpost_ref, gnext_ref, x2_ref, *rest):
    acc_ref = rest[-1]
    f = pl.program_id(1)

    @pl.when(f == 0)
    def _():
        acc_ref[...] = jnp.zeros(acc_ref.shape, F32)

    a = jnp.dot(xn_ref[...], w1_ref[...], preferred_element_type=F32)
    a = jnp.maximum(a, 0.0)
    a = (a * a).astype(BF16)
    acc_ref[...] += jnp.dot(a, w2_ref[...], preferred_element_type=F32)

    @pl.when(f == pl.num_programs(1) - 1)
    def _():
        x2 = x_ref[...] + _rms(acc_ref[...], gpost_ref[...])
        x2_ref[...] = x2
        if len(rest) == 2:
            rest[0][...] = _rms(x2, gnext_ref[...]).astype(rest[0].dtype)


def _mlp(xn2d, w1_stack, w2_stack, layer, x2d, gpost, gnext, emit_next, tm, tf):
    m, d = xn2d.shape
    ff = w1_stack.shape[-1]
    row_spec = pl.BlockSpec((tm, d), lambda i, f: (i, 0))
    out_specs = [row_spec] + ([row_spec] if emit_next else [])
    out_shape = [jax.ShapeDtypeStruct((m, d), F32)] + (
        [jax.ShapeDtypeStruct((m, d), BF16)] if emit_next else [])
    return pl.pallas_call(
        _mlp_kernel,
        grid=(m // tm, ff // tf),
        in_specs=[
            row_spec,
            pl.BlockSpec((None, d, tf), lambda i, f: (layer, 0, f)),
            pl.BlockSpec((None, tf, d), lambda i, f: (layer, f, 0)),
            row_spec,
            _resident((1, d)), _resident((1, d)),
        ],
        out_specs=out_specs,
        out_shape=out_shape,
        scratch_shapes=[pltpu.VMEM((tm, d), F32)],
        compiler_params=_cparams(("parallel", "arbitrary")),
        name="sq_relu_mlp",
    )(xn2d, w1_stack, w2_stack, x2d, gpost, gnext)


GATE_TILE = 3 * LANES


def _gate_windows(c, nblk):
    blk = c // nblk
    wins = []
    for j in range(c // GATE_TILE):
        c0, c1 = j * GATE_TILE, (j + 1) * GATE_TILE
        n0, n1 = c0 // blk, (c1 - 1) // blk
        lo = (n0 * blk) // LANES * LANES
        hi = -(-((n1 + 1) * blk) // LANES) * LANES
        wins.append((lo, hi))
    return tuple(wins)


def _lru_core_kernel(*refs, wins, ts, lane_chunks):
    nw = len(wins)
    (xn_ref, wx_ref, bx_ref, wy_ref, by_ref, cw_ref, cb_ref, bga_ref, bgi_ref, lam_ref) = refs[:10]
    wg_refs = refs[10:10 + nw]
    g_ref = refs[10 + nw]
    pu_s, hist_s, a_s, b_s, hp_s, hc_s = refs[11 + nw:]
    c = g_ref.shape[-1]
    nl = c // LANES
    seg = ts // SUBLANES
    hist = CONV_WIDTH - 1
    t = pl.program_id(1)

    @pl.when(t == 0)
    def _():
        hist_s[...] = jnp.zeros(hist_s.shape, F32)
        hc_s[...] = jnp.zeros(hc_s.shape, F32)

    xn = xn_ref[...]
    u_pre = jnp.dot(xn, wx_ref[...], preferred_element_type=F32) + bx_ref[...]
    for l in range(nl):
        for i in range(SUBLANES):
            pu_s[l, pl.ds(i, seg, stride=SUBLANES), :] = (
                u_pre[i * seg:(i + 1) * seg, l * LANES:(l + 1) * LANES])
    up = jnp.concatenate([pu_s[l] for l in range(nl)], axis=1)

    row8 = lax.broadcasted_iota(jnp.int32, (SUBLANES, c), 0)
    lead = []
    for g in range(hist):
        lo = (seg - hist + g) * SUBLANES
        cur = up[lo:lo + SUBLANES, :]
        prev = hist_s[g * SUBLANES:(g + 1) * SUBLANES, :]
        lead.append(pltpu.roll(jnp.where(row8 == SUBLANES - 1, prev, cur), 1, axis=0))
    hist_s[...] = up[(seg - hist) * SUBLANES:, :]
    ext = jnp.concatenate(lead + [up], axis=0)
    cw = cw_ref[...]
    u = cb_ref[...] + cw[hist:hist + 1, :] * up
    for kk in range(1, CONV_WIDTH):
        lo = (hist - kk) * SUBLANES
        u = u + cw[hist - kk:hist - kk + 1, :] * ext[lo:lo + ts, :]

    ub = u.astype(BF16)
    za, zi = [], []
    for j, (lo, hi) in enumerate(wins):
        z = jnp.dot(ub[:, lo:hi], wg_refs[j][...], preferred_element_type=F32)
        za.append(z[:, :GATE_TILE])
        zi.append(z[:, GATE_TILE:])
    r = _sigmoid(jnp.concatenate(za, axis=1) + bga_ref[...])
    gi = _sigmoid(jnp.concatenate(zi, axis=1) + bgi_ref[...])
    lam = lam_ref[...]
    log_sig = jnp.minimum(lam, 0.0) - jnp.log1p(jnp.exp(-jnp.abs(lam)))
    log_a = r * (LRU_C * log_sig)
    a = jnp.exp(log_a)
    a_s[...] = a
    z = -jnp.tanh(log_a) * (a * a + 1.0)
    root = jnp.where(z > 0.0, z * lax.rsqrt(z), 0.0)
    b_s[...] = root * (gi * u)

    y = _gelu_tanh(jnp.dot(xn, wy_ref[...], preferred_element_type=F32) + by_ref[...])
    for l in range(nl):
        pu_s[l] = y[:, l * LANES:(l + 1) * LANES]

    def group(ref, j, c0, c1):
        return ref[j * SUBLANES:(j + 1) * SUBLANES, c0:c1]

    tot_a, tot_h = [], []
    for c0, c1 in lane_chunks:
        def totals(j, carry, c0=c0, c1=c1):
            pa, ph = carry
            av = group(a_s, j, c0, c1)
            return av * pa, av * ph + group(b_s, j, c0, c1)

        carry = (jnp.ones((SUBLANES, c1 - c0), F32), jnp.zeros((SUBLANES, c1 - c0), F32))
        for j in range(seg):
            carry = totals(j, carry)
        pa, ph = carry
        tot_a.append(pa)
        tot_h.append(ph)
    tot_a = jnp.concatenate(tot_a, axis=1)
    tot_h = jnp.concatenate(tot_h, axis=1)

    state = hc_s[0:1, :]
    h_in = jnp.zeros((SUBLANES, c), F32)
    for i in range(SUBLANES):
        h_in = jnp.where(row8 == i, state, h_in)
        state = tot_a[i:i + 1, :] * state + tot_h[i:i + 1, :]
    hc_s[0:1, :] = state

    for c0, c1 in lane_chunks:
        def recur(j, h, c0=c0, c1=c1):
            h = group(a_s, j, c0, c1) * h + group(b_s, j, c0, c1)
            for l in range(c0 // LANES, c1 // LANES):
                hp_s[l, j * SUBLANES:(j + 1) * SUBLANES, :] = h[:, l * LANES - c0:(l + 1) * LANES - c0]
            return h

        h = h_in[:, c0:c1]
        for j in range(seg):
            h = recur(j, h)

    for i in range(SUBLANES):
        h_seg = jnp.concatenate(
            [hp_s[l, pl.ds(i, seg, stride=SUBLANES), :] for l in range(nl)], axis=1)
        rows = slice(i * seg, (i + 1) * seg)
        y_seg = jnp.concatenate([pu_s[l, rows, :] for l in range(nl)], axis=1)
        g_ref[rows, :] = (h_seg * y_seg).astype(g_ref.dtype)


def _lru_core(xn2d, wx_stack, bx, wy_stack, by, layer, b, cw, cb, bga, bgi, lam, wgs, wins, ts):
    m, d = xn2d.shape
    c = wx_stack.shape[-1]
    s = m // b
    nt = s // ts
    nl = c // LANES
    step = max(1, nl // 3) * LANES
    lane_chunks = tuple((c0, min(c, c0 + step)) for c0 in range(0, c, step))
    kern = functools.partial(_lru_core_kernel, wins=wins, ts=ts, lane_chunks=lane_chunks)
    row_spec = pl.BlockSpec((ts, c), lambda bi, t: (bi * nt + t, 0))
    return pl.pallas_call(
        kern,
        grid=(b, s // ts),
        in_specs=[pl.BlockSpec((ts, d), lambda bi, t: (bi * nt + t, 0)),
                  _layer(wx_stack.shape, layer), _resident((1, c)),
                  _layer(wy_stack.shape, layer), _resident((1, c)),
                  _resident(cw.shape), _resident((1, c)),
                  _resident((1, c)), _resident((1, c)), _resident((1, c))]
                 + [_resident(w.shape) for w in wgs],
        out_specs=row_spec,
        out_shape=jax.ShapeDtypeStruct((m, c), BF16),
        scratch_shapes=[pltpu.VMEM((nl, ts, LANES), F32),
                        pltpu.VMEM(((CONV_WIDTH - 1) * SUBLANES, c), F32),
                        pltpu.VMEM((ts, c), F32), pltpu.VMEM((ts, c), F32),
                        pltpu.VMEM((nl, ts, LANES), F32), pltpu.VMEM((SUBLANES, c), F32)],
        compiler_params=_cparams(("parallel", "arbitrary")),
        name="lru_core",
    )(xn2d, wx_stack, bx, wy_stack, by, cw, cb, bga, bgi, lam, *wgs)


def _prep_mla(w_dq, w_uq, w_dkv, w_ukv):
    d = w_dq.shape[0]
    q_lora = w_dq.shape[1]
    kv_lora = w_dkv.shape[1] - QK_ROPE
    half = QK_ROPE // 2
    zpad = jnp.zeros((d, LANES - QK_ROPE), w_dkv.dtype)
    kr = w_dkv[:, kv_lora:]
    kr_rot = jnp.concatenate([-kr[:, half:], kr[:, :half]], axis=1)
    wa = jnp.concatenate([w_dq, w_dkv[:, :kv_lora], kr, zpad, kr_rot, zpad], axis=1).astype(BF16)

    wq = w_uq.reshape(q_lora, MLA_HEADS, QK_HEAD)
    zq = jnp.zeros((q_lora, MLA_HEADS, LANES - QK_ROPE), w_uq.dtype)
    qr = wq[:, :, QK_NOPE:]
    qr_rot = jnp.concatenate([-qr[:, :, half:], qr[:, :, :half]], axis=2)
    wqn = wq[:, :, :QK_NOPE].reshape(q_lora, MLA_HEADS * LANES).astype(BF16)
    wqr = jnp.concatenate([qr, zq], axis=2).reshape(q_lora, MLA_HEADS * LANES).astype(BF16)
    wqrr = jnp.concatenate([qr_rot, zq], axis=2).reshape(q_lora, MLA_HEADS * LANES).astype(BF16)

    wkv = w_ukv.reshape(kv_lora, MLA_HEADS, QK_NOPE + V_HEAD)
    wkn = wkv[:, :, :QK_NOPE].reshape(kv_lora, MLA_HEADS * QK_NOPE).astype(BF16)
    wvt = wkv[:, :, QK_NOPE:].reshape(kv_lora, MLA_HEADS * V_HEAD).T.astype(BF16)
    return wa, wqn, wqr, wqrr, wkn, wvt


def _prep_gates(w_ga, w_gi, wins):
    nblk, blk, _ = w_ga.shape

    def slab(w, j, lo, hi):
        c0, c1 = j * GATE_TILE, (j + 1) * GATE_TILE
        out = jnp.zeros((hi - lo, GATE_TILE), w.dtype)
        for n in range(c0 // blk, (c1 - 1) // blk + 1):
            q0, q1 = max(c0, n * blk), min(c1, (n + 1) * blk)
            piece = w[n][:, q0 - n * blk:q1 - n * blk]
            r0 = n * blk - lo
            out = out + jnp.pad(piece, ((r0, hi - lo - r0 - blk), (q0 - c0, c1 - q1)))
        return out

    return [jnp.concatenate([slab(w_ga, j, lo, hi), slab(w_gi, j, lo, hi)], axis=1).astype(BF16)
            for j, (lo, hi) in enumerate(wins)]


def _pick(n, pref):
    t = min(n, pref)
    while n % t:
        t //= 2
    return t


def kernel(x, positions, mix_pre_g, mix_post_g, mlp_pre_g, mlp_post_g, mla_w_dq, mla_g_q, mla_w_uq, mla_w_dkv, mla_g_kv, mla_w_ukv, mla_w_o, lru_w_y, lru_b_y, lru_w_x, lru_b_x, lru_conv_w, lru_conv_b, lru_w_ga, lru_b_ga, lru_w_gi, lru_b_gi, lru_lam, lru_w_out, lru_b_out, mlp_w1, mlp_w2):
    b, s, d = x.shape
    depth = mix_pre_g.shape[0]
    m = b * s
    c = lru_w_y.shape[-1]
    tm = _pick(m, 512)
    tq = _pick(s, 2048)
    tk = tq // 4
    ts = _pick(s, 256)
    tf = _pick(mlp_w1.shape[-1], 1024)
    wins = _gate_windows(c, LRU_BLOCKS)

    half = QK_ROPE // 2
    inv_freq = ROPE_THETA ** (-jnp.arange(half, dtype=F32) / half)
    invf = jnp.tile(inv_freq, LANES // half)[None, :]
    pos2d = positions.reshape(m, 1)
    zero_bias = jnp.zeros((1, d), F32)
    w1_b, w2_b = mlp_w1.astype(BF16), mlp_w2.astype(BF16)
    wo_b = mla_w_o.astype(BF16)
    wy_b, wx_b, wout_b = lru_w_y.astype(BF16), lru_w_x.astype(BF16), lru_w_out.astype(BF16)

    def row(v):
        return v[None, :].astype(F32)

    x2d = x.reshape(m, d)
    xn = _prenorm(x2d, row(mix_pre_g[0]), tm)
    for i in range(depth):
        j = i // N_MIXERS
        if i % N_MIXERS == 0:
            wa, wqn, wqr, wqrr, wkn, wvt = _prep_mla(
                mla_w_dq[j], mla_w_uq[j], mla_w_dkv[j], mla_w_ukv[j])
            q, k, vt = _mla_proj(xn, pos2d, b, invf, wa, row(mla_g_q[j]), row(mla_g_kv[j]),
                                 wqn, wqr, wqrr, wkn, wvt, tk)
            o = _attention(q, k, vt, tq, tk)
            x2d, xn = _out_proj(o, wo_b, j, zero_bias, x2d,
                                row(mix_post_g[i]), row(mlp_pre_g[i]), tm)
        else:
            wgs = _prep_gates(lru_w_ga[j], lru_w_gi[j], wins)
            g = _lru_core(xn, wx_b, row(lru_b_x[j]), wy_b, row(lru_b_y[j]), j, b, lru_conv_w[j],
                          row(lru_conv_b[j]), row(lru_b_ga[j]), row(lru_b_gi[j]),
                          row(lru_lam[j]), wgs, wins, ts)
            x2d, xn = _out_proj(g, wout_b, j, row(lru_b_out[j]), x2d,
                                row(mix_post_g[i]), row(mlp_pre_g[i]), tm)
        last = i + 1 == depth
        g_next = mix_pre_g[0] if last else mix_pre_g[i + 1]
        outs = _mlp(xn, w1_b, w2_b, i, x2d, row(mlp_post_g[i]), row(g_next), not last, tm, tf)
        x2d = outs[0]
        xn = None if last else outs[1]
    return x2d.reshape(b, s, d)
```

```python
import functools
import math

import jax
import jax.numpy as jnp
from jax import lax
from jax.experimental import pallas as pl
from jax.experimental.pallas import tpu as pltpu

MLA_HEADS = 16
QK_NOPE = 128
QK_ROPE = 64
V_HEAD = 128
QK_HEAD = QK_NOPE + QK_ROPE
ROPE_THETA = 10000.0
CHUNK = 64
LRU_BLOCKS = 16
CONV_WIDTH = 4
LRU_C = 8.0
EPS = 1e-6
N_MIXERS = 2

LANES = 128
SUBLANES = 8
BF16_SUBLANES = 16
V7X_SCOPED_VMEM_BYTES = 60000 * 1024

HEAD_PAD = 2 * LANES
VT_ROWS = V_HEAD + BF16_SUBLANES
NEG_BIG = -1e30

F32 = jnp.float32
BF16 = jnp.bfloat16


def _cparams(semantics):
    return pltpu.CompilerParams(dimension_semantics=semantics,
                                vmem_limit_bytes=V7X_SCOPED_VMEM_BYTES)


def _resident(shape):
    nd = len(shape)
    return pl.BlockSpec(shape, lambda *_: (0,) * nd, pipeline_mode=pl.Buffered(1))


def _layer(shape, layer):
    nd = len(shape) - 1
    return pl.BlockSpec((None,) + tuple(shape[1:]), lambda *_: (layer,) + (0,) * nd,
                        pipeline_mode=pl.Buffered(1))


def _rms(x, g):
    y = x * lax.rsqrt(jnp.mean(x * x, axis=-1, keepdims=True) + EPS)
    return y * g


def _sigmoid(x):
    return 0.5 * jnp.tanh(0.5 * x) + 0.5


def _gelu_tanh(x):
    c = math.sqrt(2.0 / math.pi)
    return 0.5 * x * (1.0 + jnp.tanh(c * (x + 0.044715 * (x * x * x))))


def _prenorm_kernel(x_ref, g_ref, o_ref):
    o_ref[...] = _rms(x_ref[...], g_ref[...]).astype(o_ref.dtype)


def _prenorm(x2d, g, tm):
    m, d = x2d.shape
    return pl.pallas_call(
        _prenorm_kernel,
        grid=(m // tm,),
        in_specs=[pl.BlockSpec((tm, d), lambda i: (i, 0)), _resident((1, d))],
        out_specs=pl.BlockSpec((tm, d), lambda i: (i, 0)),
        out_shape=jax.ShapeDtypeStruct((m, d), BF16),
        compiler_params=_cparams(("parallel",)),
        name="prenorm",
    )(x2d, g)


def _mla_proj_kernel(xn_ref, pos_ref, invf_ref, wa_ref, gq_ref, gkv_ref,
                     wqn_ref, wqr_ref, wqrr_ref, wkn_ref, wvt_ref,
                     q_ref, k_ref, vt_ref, *, qscale, q_lora, kv_lora, head_group):
    xn = xn_ref[...]
    a = jnp.dot(xn, wa_ref[...], preferred_element_type=F32)
    cq = _rms(a[:, :q_lora], gq_ref[...]).astype(BF16)
    ckv = _rms(a[:, q_lora:q_lora + kv_lora], gkv_ref[...]).astype(BF16)
    ang = pos_ref[...].astype(F32) * invf_ref[...]
    cos = jnp.cos(ang)
    sin = jnp.sin(ang)
    r0 = q_lora + kv_lora
    kr = (a[:, r0:r0 + LANES] * cos + a[:, r0 + LANES:r0 + 2 * LANES] * sin).astype(BF16)
    cs = cos * qscale
    sn = sin * qscale
    ones = jnp.ones((VT_ROWS - V_HEAD, vt_ref.shape[-1]), BF16)
    gw = head_group * LANES
    for g in range(MLA_HEADS // head_group):
        cols = slice(g * gw, (g + 1) * gw)
        qn = jnp.dot(cq, wqn_ref[:, cols], preferred_element_type=F32)
        qr = jnp.dot(cq, wqr_ref[:, cols], preferred_element_type=F32)
        qrr = jnp.dot(cq, wqrr_ref[:, cols], preferred_element_type=F32)
        kn = jnp.dot(ckv, wkn_ref[:, cols], preferred_element_type=F32)
        vt = lax.dot_general(wvt_ref[cols, :], ckv, (((1,), (1,)), ((), ())),
                             preferred_element_type=F32)
        for j in range(head_group):
            h = g * head_group + j
            sl = slice(j * LANES, (j + 1) * LANES)
            q_ref[0, h, :, 0:LANES] = (qn[:, sl] * qscale).astype(BF16)
            q_ref[0, h, :, LANES:HEAD_PAD] = (qr[:, sl] * cs + qrr[:, sl] * sn).astype(BF16)
            k_ref[0, h, :, 0:LANES] = kn[:, sl].astype(BF16)
            k_ref[0, h, :, LANES:HEAD_PAD] = kr
            vt_ref[0, h, 0, 0:V_HEAD, :] = vt[sl, :].astype(BF16)
            vt_ref[0, h, 0, V_HEAD:VT_ROWS, :] = ones


def _mla_proj(xn2d, pos2d, b, invf, wa, gq, gkv, wqn, wqr, wqrr, wkn, wvt, tk):
    m, d = xn2d.shape
    s = m // b
    nt = s // tk
    q_lora = gq.shape[-1]
    kv_lora = gkv.shape[-1]
    qscale = (QK_HEAD ** -0.5) * math.log2(math.e)
    kern = functools.partial(_mla_proj_kernel, qscale=qscale, q_lora=q_lora,
                             kv_lora=kv_lora, head_group=4)
    hq = MLA_HEADS
    return pl.pallas_call(
        kern,
        grid=(b, s // tk),
        in_specs=[
            pl.BlockSpec((tk, d), lambda bi, i: (bi * nt + i, 0)),
            pl.BlockSpec((tk, 1), lambda bi, i: (bi * nt + i, 0)),
            _resident(invf.shape), _resident(wa.shape), _resident(gq.shape),
            _resident(gkv.shape), _resident(wqn.shape), _resident(wqr.shape),
            _resident(wqrr.shape), _resident(wkn.shape), _resident(wvt.shape),
        ],
        out_specs=[
            pl.BlockSpec((1, hq, tk, HEAD_PAD), lambda bi, i: (bi, 0, i, 0)),
            pl.BlockSpec((1, hq, tk, HEAD_PAD), lambda bi, i: (bi, 0, i, 0)),
            pl.BlockSpec((1, hq, 1, VT_ROWS, tk), lambda bi, i: (bi, 0, i, 0, 0)),
        ],
        out_shape=[
            jax.ShapeDtypeStruct((b, hq, s, HEAD_PAD), BF16),
            jax.ShapeDtypeStruct((b, hq, s, HEAD_PAD), BF16),
            jax.ShapeDtypeStruct((b, hq, s // tk, VT_ROWS, tk), BF16),
        ],
        compiler_params=_cparams(("parallel", "parallel")),
        name="mla_proj",
    )(xn2d, pos2d, invf, wa, gq, gkv, wqn, wqr, wqrr, wkn, wvt)


def _attn_kernel(q_ref, k_ref, vt_ref, o_ref, acc_ref, m_ref, s_ref, mx_ref, *, tq, tk, ct):
    qi = pl.program_id(2)
    r = tq // tk
    acc_ref[...] = jnp.zeros(acc_ref.shape, F32)
    m_ref[...] = jnp.full(m_ref.shape, NEG_BIG, F32)
    n_full = qi * r

    def qk(ki, slot, c0):
        k0 = pl.multiple_of(ki * tk, tk)
        kb = k_ref[0, 0, pl.ds(k0, tk), :]
        for c in range(c0, tq, ct):
            s = lax.dot_general(kb, q_ref[0, 0, c:c + ct, :], (((1,), (1,)), ((), ())),
                                preferred_element_type=F32)
            s_ref[slot, :, c:c + ct] = s
            mx_ref[slot, 0:1, c:c + ct] = jnp.max(s, axis=0, keepdims=True)

    def softmax_pv(ki, slot, j):
        c0 = 0 if j is None else j * tk
        vt = vt_ref[0, 0, ki]
        for c in range(c0, tq, ct):
            s = s_ref[slot, :, c:c + ct]
            if j is None or c >= c0 + tk:
                m_chunk = mx_ref[slot, 0:1, c:c + ct]
            else:
                kc = (j * tk + lax.broadcasted_iota(jnp.int32, s.shape, 0)) // CHUNK
                qc = (c + lax.broadcasted_iota(jnp.int32, s.shape, 1)) // CHUNK
                s = jnp.where(kc <= qc, s, NEG_BIG)
                m_chunk = jnp.max(s, axis=0, keepdims=True)
            m_old = m_ref[0:1, c:c + ct]
            m_new = jnp.maximum(m_old, m_chunk)
            alpha = jnp.exp2(m_old - m_new)
            p = jnp.exp2(s - m_new).astype(BF16)
            pv = jnp.dot(vt, p, preferred_element_type=F32)
            acc_ref[:, c:c + ct] = acc_ref[:, c:c + ct] * alpha + pv
            m_ref[0:1, c:c + ct] = m_new

    qk(0, 0, 0)

    def group(g, carry):
        for u in range(r):
            qk(g * r + u + 1, (u + 1) % 2, 0)
            softmax_pv(g * r + u, u % 2, None)
        return carry

    lax.fori_loop(0, qi, group, 0)
    for j in range(r):
        if j + 1 < r:
            qk(n_full + j + 1, (j + 1) % 2, (j + 1) * tk)
        softmax_pv(n_full + j, j % 2, j)
    acc = acc_ref[...]
    o = acc[0:V_HEAD, :] / acc[V_HEAD:V_HEAD + 1, :]
    o_ref[...] = o.T.astype(o_ref.dtype)


def _attention(q, k, vt, tq, tk):
    b, h, s, _ = q.shape
    assert (tq // tk) % 2 == 0, "key chunks are processed in pairs"
    kern = functools.partial(_attn_kernel, tq=tq, tk=tk, ct=tk)
    return pl.pallas_call(
        kern,
        grid=(b, h, s // tq),
        in_specs=[
            pl.BlockSpec((1, 1, tq, HEAD_PAD), lambda bi, hi, qi: (bi, hi, qi, 0)),
            pl.BlockSpec((1, 1, s, HEAD_PAD), lambda bi, hi, qi: (bi, hi, 0, 0)),
            pl.BlockSpec((1, 1, s // tk, VT_ROWS, tk), lambda bi, hi, qi: (bi, hi, 0, 0, 0)),
        ],
        out_specs=pl.BlockSpec((tq, V_HEAD), lambda bi, hi, qi: (bi * (s // tq) + qi, hi)),
        out_shape=jax.ShapeDtypeStruct((b * s, h * V_HEAD), BF16),
        scratch_shapes=[pltpu.VMEM((VT_ROWS, tq), F32), pltpu.VMEM((SUBLANES, tq), F32),
                        pltpu.VMEM((2, tk, tq), F32), pltpu.VMEM((2, SUBLANES, tq), F32)],
        compiler_params=_cparams(("parallel", "parallel", "arbitrary")),
        name="mla_attention",
    )(q, k, vt)


def _out_proj_kernel(a_ref, w_ref, b_ref, x_ref, gpost_ref, gnext_ref, x1_ref, xn_ref, *, parts):
    rows = a_ref.shape[0] // parts
    for p in range(parts):
        rs = slice(p * rows, (p + 1) * rows)
        h = jnp.dot(a_ref[rs, :], w_ref[...], preferred_element_type=F32) + b_ref[...]
        x1 = x_ref[rs, :] + _rms(h, gpost_ref[...])
        x1_ref[rs, :] = x1
        xn_ref[rs, :] = _rms(x1, gnext_ref[...]).astype(xn_ref.dtype)


def _out_proj(a2d, w_stack, layer, bias, x2d, gpost, gnext, tm):
    m, kdim = a2d.shape
    d = w_stack.shape[-1]
    return pl.pallas_call(
        functools.partial(_out_proj_kernel, parts=4),
        grid=(m // tm,),
        in_specs=[
            pl.BlockSpec((tm, kdim), lambda i: (i, 0)),
            _layer(w_stack.shape, layer), _resident((1, d)),
            pl.BlockSpec((tm, d), lambda i: (i, 0)),
            _resident((1, d)), _resident((1, d)),
        ],
        out_specs=[pl.BlockSpec((tm, d), lambda i: (i, 0)),
                   pl.BlockSpec((tm, d), lambda i: (i, 0))],
        out_shape=[jax.ShapeDtypeStruct((m, d), F32), jax.ShapeDtypeStruct((m, d), BF16)],
        compiler_params=_cparams(("parallel",)),
        name="out_proj",
    )(a2d, w_stack, bias, x2d, gpost, gnext)


def _mlp_kernel(xn_ref, w1_ref, w2_ref, x_ref, gpost_ref, gnext_ref, x2_ref, *rest):
    acc_ref = rest[-1]
    f = pl.program_id(1)

    @pl.when(f == 0)
    def _():
        acc_ref[...] = jnp.zeros(acc_ref.shape, F32)

    a = jnp.dot(xn_ref[...], w1_ref[...], preferred_element_type=F32)
    a = jnp.maximum(a, 0.0)
    a = (a * a).astype(BF16)
    acc_ref[...] += jnp.dot(a, w2_ref[...], preferred_element_type=F32)

    @pl.when(f == pl.num_programs(1) - 1)
    def _():
        x2 = x_ref[...] + _rms(acc_ref[...], gpost_ref[...])
        x2_ref[...] = x2
        if len(rest) == 2:
            rest[0][...] = _rms(x2, gnext_ref[...]).astype(rest[0].dtype)


def _mlp(xn2d, w1_stack, w2_stack, layer, x2d, gpost, gnext, emit_next, tm, tf):
    m, d = xn2d.shape
    ff = w1_stack.shape[-1]
    row_spec = pl.BlockSpec((tm, d), lambda i, f: (i, 0))
    out_specs = [row_spec] + ([row_spec] if emit_next else [])
    out_shape = [jax.ShapeDtypeStruct((m, d), F32)] + (
        [jax.ShapeDtypeStruct((m, d), BF16)] if emit_next else [])
    return pl.pallas_call(
        _mlp_kernel,
        grid=(m // tm, ff // tf),
        in_specs=[
            row_spec,
            pl.BlockSpec((None, d, tf), lambda i, f: (layer, 0, f)),
            pl.BlockSpec((None, tf, d), lambda i, f: (layer, f, 0)),
            row_spec,
            _resident((1, d)), _resident((1, d)),
        ],
        out_specs=out_specs,
        out_shape=out_shape,
        scratch_shapes=[pltpu.VMEM((tm, d), F32)],
        compiler_params=_cparams(("parallel", "arbitrary")),
        name="sq_relu_mlp",
    )(xn2d, w1_stack, w2_stack, x2d, gpost, gnext)


GATE_TILE = 3 * LANES


def _gate_windows(c, nblk):
    blk = c // nblk
    wins = []
    for j in range(c // GATE_TILE):
        c0, c1 = j * GATE_TILE, (j + 1) * GATE_TILE
        n0, n1 = c0 // blk, (c1 - 1) // blk
        lo = (n0 * blk) // LANES * LANES
        hi = -(-((n1 + 1) * blk) // LANES) * LANES
        wins.append((lo, hi))
    return tuple(wins)


def _lru_core_kernel(*refs, wins, ts, lane_chunks):
    nw = len(wins)
    (xn_ref, wx_ref, bx_ref, wy_ref, by_ref, cw_ref, cb_ref, bga_ref, bgi_ref, lam_ref) = refs[:10]
    wg_refs = refs[10:10 + nw]
    g_ref = refs[10 + nw]
    pu_s, hist_s, a_s, b_s, hp_s, hc_s = refs[11 + nw:]
    c = g_ref.shape[-1]
    nl = c // LANES
    seg = ts // SUBLANES
    hist = CONV_WIDTH - 1
    t = pl.program_id(1)

    @pl.when(t == 0)
    def _():
        hist_s[...] = jnp.zeros(hist_s.shape, F32)
        hc_s[...] = jnp.zeros(hc_s.shape, F32)

    xn = xn_ref[...]
    u_pre = jnp.dot(xn, wx_ref[...], preferred_element_type=F32) + bx_ref[...]
    for l in range(nl):
        for i in range(SUBLANES):
            pu_s[l, pl.ds(i, seg, stride=SUBLANES), :] = (
                u_pre[i * seg:(i + 1) * seg, l * LANES:(l + 1) * LANES])
    up = jnp.concatenate([pu_s[l] for l in range(nl)], axis=1)

    row8 = lax.broadcasted_iota(jnp.int32, (SUBLANES, c), 0)
    lead = []
    for g in range(hist):
        lo = (seg - hist + g) * SUBLANES
        cur = up[lo:lo + SUBLANES, :]
        prev = hist_s[g * SUBLANES:(g + 1) * SUBLANES, :]
        lead.append(pltpu.roll(jnp.where(row8 == SUBLANES - 1, prev, cur), 1, axis=0))
    hist_s[...] = up[(seg - hist) * SUBLANES:, :]
    ext = jnp.concatenate(lead + [up], axis=0)
    cw = cw_ref[...]
    u = cb_ref[...] + cw[hist:hist + 1, :] * up
    for kk in range(1, CONV_WIDTH):
        lo = (hist - kk) * SUBLANES
        u = u + cw[hist - kk:hist - kk + 1, :] * ext[lo:lo + ts, :]

    ub = u.astype(BF16)
    za, zi = [], []
    for j, (lo, hi) in enumerate(wins):
        z = jnp.dot(ub[:, lo:hi], wg_refs[j][...], preferred_element_type=F32)
        za.append(z[:, :GATE_TILE])
        zi.append(z[:, GATE_TILE:])
    r = _sigmoid(jnp.concatenate(za, axis=1) + bga_ref[...])
    gi = _sigmoid(jnp.concatenate(zi, axis=1) + bgi_ref[...])
    lam = lam_ref[...]
    log_sig = jnp.minimum(lam, 0.0) - jnp.log1p(jnp.exp(-jnp.abs(lam)))
    log_a = r * (LRU_C * log_sig)
    a = jnp.exp(log_a)
    a_s[...] = a
    z = -jnp.tanh(log_a) * (a * a + 1.0)
    root = jnp.where(z > 0.0, z * lax.rsqrt(z), 0.0)
    b_s[...] = root * (gi * u)

    y = _gelu_tanh(jnp.dot(xn, wy_ref[...], preferred_element_type=F32) + by_ref[...])
    for l in range(nl):
        pu_s[l] = y[:, l * LANES:(l + 1) * LANES]

    def group(ref, j, c0, c1):
        return ref[j * SUBLANES:(j + 1) * SUBLANES, c0:c1]

    tot_a, tot_h = [], []
    for c0, c1 in lane_chunks:
        def totals(j, carry, c0=c0, c1=c1):
            pa, ph = carry
            av = group(a_s, j, c0, c1)
            return av * pa, av * ph + group(b_s, j, c0, c1)

        carry = (jnp.ones((SUBLANES, c1 - c0), F32), jnp.zeros((SUBLANES, c1 - c0), F32))
        for j in range(seg):
            carry = totals(j, carry)
        pa, ph = carry
        tot_a.append(pa)
        tot_h.append(ph)
    tot_a = jnp.concatenate(tot_a, axis=1)
    tot_h = jnp.concatenate(tot_h, axis=1)

    state = hc_s[0:1, :]
    h_in = jnp.zeros((SUBLANES, c), F32)
    for i in range(SUBLANES):
        h_in = jnp.where(row8 == i, state, h_in)
        state = tot_a[i:i + 1, :] * state + tot_h[i:i + 1, :]
    hc_s[0:1, :] = state

    for c0, c1 in lane_chunks:
        def recur(j, h, c0=c0, c1=c1):
            h = group(a_s, j, c0, c1) * h + group(b_s, j, c0, c1)
            for l in range(c0 // LANES, c1 // LANES):
                hp_s[l, j * SUBLANES:(j + 1) * SUBLANES, :] = h[:, l * LANES - c0:(l + 1) * LANES - c0]
            return h

        h = h_in[:, c0:c1]
        for j in range(seg):
            h = recur(j, h)

    for i in range(SUBLANES):
        h_seg = jnp.concatenate(
            [hp_s[l, pl.ds(i, seg, stride=SUBLANES), :] for l in range(nl)], axis=1)
        rows = slice(i * seg, (i + 1) * seg)
        y_seg = jnp.concatenate([pu_s[l, rows, :] for l in range(nl)], axis=1)
        g_ref[rows, :] = (h_seg * y_seg).astype(g_ref.dtype)


def _lru_core(xn2d, wx_stack, bx, wy_stack, by, layer, b, cw, cb, bga, bgi, lam, wgs, wins, ts):
    m, d = xn2d.shape
    c = wx_stack.shape[-1]
    s = m // b
    nt = s // ts
    nl = c // LANES
    step = max(1, nl // 3) * LANES
    lane_chunks = tuple((c0, min(c, c0 + step)) for c0 in range(0, c, step))
    kern = functools.partial(_lru_core_kernel, wins=wins, ts=ts, lane_chunks=lane_chunks)
    row_spec = pl.BlockSpec((ts, c), lambda bi, t: (bi * nt + t, 0))
    return pl.pallas_call(
        kern,
        grid=(b, s // ts),
        in_specs=[pl.BlockSpec((ts, d), lambda bi, t: (bi * nt + t, 0)),
                  _layer(wx_stack.shape, layer), _resident((1, c)),
                  _layer(wy_stack.shape, layer), _resident((1, c)),
                  _resident(cw.shape), _resident((1, c)),
                  _resident((1, c)), _resident((1, c)), _resident((1, c))]
                 + [_resident(w.shape) for w in wgs],
        out_specs=row_spec,
        out_shape=jax.ShapeDtypeStruct((m, c), BF16),
        scratch_shapes=[pltpu.VMEM((nl, ts, LANES), F32),
                        pltpu.VMEM(((CONV_WIDTH - 1) * SUBLANES, c), F32),
                        pltpu.VMEM((ts, c), F32), pltpu.VMEM((ts, c), F32),
                        pltpu.VMEM((nl, ts, LANES), F32), pltpu.VMEM((SUBLANES, c), F32)],
        compiler_params=_cparams(("parallel", "arbitrary")),
        name="lru_core",
    )(xn2d, wx_stack, bx, wy_stack, by, cw, cb, bga, bgi, lam, *wgs)


def _prep_mla(w_dq, w_uq, w_dkv, w_ukv):
    d = w_dq.shape[0]
    q_lora = w_dq.shape[1]
    kv_lora = w_dkv.shape[1] - QK_ROPE
    half = QK_ROPE // 2
    zpad = jnp.zeros((d, LANES - QK_ROPE), w_dkv.dtype)
    kr = w_dkv[:, kv_lora:]
    kr_rot = jnp.concatenate([-kr[:, half:], kr[:, :half]], axis=1)
    wa = jnp.concatenate([w_dq, w_dkv[:, :kv_lora], kr, zpad, kr_rot, zpad], axis=1).astype(BF16)

    wq = w_uq.reshape(q_lora, MLA_HEADS, QK_HEAD)
    zq = jnp.zeros((q_lora, MLA_HEADS, LANES - QK_ROPE), w_uq.dtype)
    qr = wq[:, :, QK_NOPE:]
    qr_rot = jnp.concatenate([-qr[:, :, half:], qr[:, :, :half]], axis=2)
    wqn = wq[:, :, :QK_NOPE].reshape(q_lora, MLA_HEADS * LANES).astype(BF16)
    wqr = jnp.concatenate([qr, zq], axis=2).reshape(q_lora, MLA_HEADS * LANES).astype(BF16)
    wqrr = jnp.concatenate([qr_rot, zq], axis=2).reshape(q_lora, MLA_HEADS * LANES).astype(BF16)

    wkv = w_ukv.reshape(kv_lora, MLA_HEADS, QK_NOPE + V_HEAD)
    wkn = wkv[:, :, :QK_NOPE].reshape(kv_lora, MLA_HEADS * QK_NOPE).astype(BF16)
    wvt = wkv[:, :, QK_NOPE:].reshape(kv_lora, MLA_HEADS * V_HEAD).T.astype(BF16)
    return wa, wqn, wqr, wqrr, wkn, wvt


def _prep_gates(w_ga, w_gi, wins):
    nblk, blk, _ = w_ga.shape

    def slab(w, j, lo, hi):
        c0, c1 = j * GATE_TILE, (j + 1) * GATE_TILE
        out = jnp.zeros((hi - lo, GATE_TILE), w.dtype)
        for n in range(c0 // blk, (c1 - 1) // blk + 1):
            q0, q1 = max(c0, n * blk), min(c1, (n + 1) * blk)
            piece = w[n][:, q0 - n * blk:q1 - n * blk]
            r0 = n * blk - lo
            out = out + jnp.pad(piece, ((r0, hi - lo - r0 - blk), (q0 - c0, c1 - q1)))
        return out

    return [jnp.concatenate([slab(w_ga, j, lo, hi), slab(w_gi, j, lo, hi)], axis=1).astype(BF16)
            for j, (lo, hi) in enumerate(wins)]


def _pick(n, pref):
    t = min(n, pref)
    while n % t:
        t //= 2
    return t


def kernel(x, positions, mix_pre_g, mix_post_g, mlp_pre_g, mlp_post_g, mla_w_dq, mla_g_q, mla_w_uq, mla_w_dkv, mla_g_kv, mla_w_ukv, mla_w_o, lru_w_y, lru_b_y, lru_w_x, lru_b_x, lru_conv_w, lru_conv_b, lru_w_ga, lru_b_ga, lru_w_gi, lru_b_gi, lru_lam, lru_w_out, lru_b_out, mlp_w1, mlp_w2):
    b, s, d = x.shape
    depth = mix_pre_g.shape[0]
    m = b * s
    c = lru_w_y.shape[-1]
    tm = _pick(m, 512)
    tq = _pick(s, 4096)
    tk = tq // 8
    ts = _pick(s, 256)
    tf = _pick(mlp_w1.shape[-1], 1024)
    wins = _gate_windows(c, LRU_BLOCKS)

    half = QK_ROPE // 2
    inv_freq = ROPE_THETA ** (-jnp.arange(half, dtype=F32) / half)
    invf = jnp.tile(inv_freq, LANES // half)[None, :]
    pos2d = positions.reshape(m, 1)
    zero_bias = jnp.zeros((1, d), F32)
    w1_b, w2_b = mlp_w1.astype(BF16), mlp_w2.astype(BF16)
    wo_b = mla_w_o.astype(BF16)
    wy_b, wx_b, wout_b = lru_w_y.astype(BF16), lru_w_x.astype(BF16), lru_w_out.astype(BF16)

    def row(v):
        return v[None, :].astype(F32)

    x2d = x.reshape(m, d)
    xn = _prenorm(x2d, row(mix_pre_g[0]), tm)
    for i in range(depth):
        j = i // N_MIXERS
        if i % N_MIXERS == 0:
            wa, wqn, wqr, wqrr, wkn, wvt = _prep_mla(
                mla_w_dq[j], mla_w_uq[j], mla_w_dkv[j], mla_w_ukv[j])
            q, k, vt = _mla_proj(xn, pos2d, b, invf, wa, row(mla_g_q[j]), row(mla_g_kv[j]),
                                 wqn, wqr, wqrr, wkn, wvt, tk)
            o = _attention(q, k, vt, tq, tk)
            x2d, xn = _out_proj(o, wo_b, j, zero_bias, x2d,
                                row(mix_post_g[i]), row(mlp_pre_g[i]), tm)
        else:
            wgs = _prep_gates(lru_w_ga[j], lru_w_gi[j], wins)
            g = _lru_core(xn, wx_b, row(lru_b_x[j]), wy_b, row(lru_b_y[j]), j, b, lru_conv_w[j],
                          row(lru_conv_b[j]), row(lru_b_ga[j]), row(lru_b_gi[j]),
                          row(lru_lam[j]), wgs, wins, ts)
            x2d, xn = _out_proj(g, wout_b, j, row(lru_b_out[j]), x2d,
                                row(mix_post_g[i]), row(mlp_pre_g[i]), tm)
        last = i + 1 == depth
        g_next = mix_pre_g[0] if last else mix_pre_g[i + 1]
        outs = _mlp(xn, w1_b, w2_b, i, x2d, row(mlp_post_g[i]), row(g_next), not last, tm, tf)
        x2d = outs[0]
        xn = None if last else outs[1]
    return x2d.reshape(b, s, d)
```
